```python
import jax, jax.numpy as jnp
from jax import lax
import numpy as np

D_MODEL = 2048
BATCH = 1
SEQ = 16384
DEPTH = 2

D_MIX = D_MODEL
SB_HEADS = 8
SB_HEAD_DIM = 128
SB_WIDTH = SB_HEADS * SB_HEAD_DIM
ML_HEADS = 4
ML_QK_DIM = 128
ML_V_DIM = 256
ML_QK_WIDTH = ML_HEADS * ML_QK_DIM
ML_V_WIDTH = ML_HEADS * ML_V_DIM
CONV_WIDTH = 4
ML_CHUNK = 64
Q_BLOCK = 128
D_FF = 5632
N_IN = 3 * SB_WIDTH + 2 * ML_QK_WIDTH + 2 * ML_V_WIDTH + 2 * ML_HEADS
N_SUB = 3
N_MOD = 3
ALPHA = (2 * DEPTH) ** 0.25
BETA = (8 * DEPTH) ** -0.25
LN_EPS = 1e-5

kernel_name = "hymba_style_stickbreak_mlstm_macaron_deepnorm_adaln"


def layer_norm(x, g, b):
    xf = x.astype(jnp.float32)
    mu = jnp.mean(xf, -1, keepdims=True)
    var = jnp.mean(jnp.square(xf - mu), -1, keepdims=True)
    return ((xf - mu) * lax.rsqrt(var + LN_EPS) * g + b).astype(x.dtype)


def head_norm(y, g):
    mu = jnp.mean(y, -1, keepdims=True)
    var = jnp.mean(jnp.square(y - mu), -1, keepdims=True)
    return (y - mu) * lax.rsqrt(var + LN_EPS) * g.reshape(y.shape[2:])


def swiglu_ffn(h, w1, w2):
    a, u = jnp.split(h @ w1, 2, axis=-1)
    return (jax.nn.silu(a) * u) @ w2


def causal_depthwise_conv(u, w, b):
    K = w.shape[0]
    S = u.shape[1]
    up = jnp.pad(u, ((0, 0), (K - 1, 0), (0, 0)))
    return b + sum(up[:, k:k + S, :] * w[k] for k in range(K))


def stick_breaking_attention(q, k, v):
    B, S, H, d = q.shape
    nb = S // Q_BLOCK
    qf = q.astype(jnp.float32) * (d ** -0.5)
    kf = k.astype(jnp.float32)
    vf = v.astype(jnp.float32)
    q_blocks = qf.reshape(B, nb, Q_BLOCK, H, d).transpose(1, 0, 3, 2, 4)
    key_pos = jnp.arange(S)

    def block(args):
        i, qb = args
        z = jnp.einsum('bhqd,bshd->bhqs', qb, kf)
        q_pos = i * Q_BLOCK + jnp.arange(Q_BLOCK)
        mask = key_pos[None, :] < q_pos[:, None]
        log_1mb = jnp.where(mask, jax.nn.log_sigmoid(-z), 0.0)
        log_stick = lax.cumsum(log_1mb, axis=log_1mb.ndim - 1, reverse=True) - log_1mb
        a = jnp.where(mask, jnp.exp(jax.nn.log_sigmoid(z) + log_stick), 0.0)
        return jnp.einsum('bhqs,bshd->bqhd', a, vf)

    out = lax.map(block, (jnp.arange(nb), q_blocks))
    return out.transpose(1, 0, 2, 3, 4).reshape(B, S, H, d)


def mlstm_chunkwise(q, k, v, log_i, log_f):
    B, S, H, dk = q.shape
    dv = v.shape[-1]
    L = ML_CHUNK
    nc = S // L

    def to_chunks(t):
        t = t.reshape((B, nc, L, H) + t.shape[3:])
        return jnp.moveaxis(t, (1, 3), (0, 2))

    causal = jnp.tril(jnp.ones((L, L), dtype=bool))

    def step(carry, xs):
        C, n, m = carry
        qc, kc, vc, ic, fc = xs
        b = jnp.cumsum(fc, axis=-1)
        log_d = jnp.where(causal, b[..., :, None] - b[..., None, :] + ic[..., None, :], -jnp.inf)
        log_g = b + m[..., None]
        m_t = jnp.maximum(jnp.max(log_d, -1), log_g)
        p = jnp.exp(log_d - m_t[..., None])
        g = jnp.exp(log_g - m_t)
        s = jnp.einsum('bhtd,bhsd->bhts', qc, kc) * p
        num = jnp.einsum('bhts,bhsv->bhtv', s, vc) + g[..., None] * jnp.einsum('bhtd,bhdv->bhtv', qc, C)
        den = jnp.sum(s, -1) + g * jnp.einsum('bhtd,bhd->bht', qc, n)
        h = num / jnp.maximum(jnp.abs(den), jnp.exp(-m_t))[..., None]
        b_last = b[..., -1]
        log_w = b_last[..., None] - b + ic
        m_new = jnp.maximum(b_last + m, jnp.max(log_w, -1))
        decay = jnp.exp(b_last + m - m_new)
        w = jnp.exp(log_w - m_new[..., None])
        C = decay[..., None, None] * C + jnp.einsum('bhs,bhsd,bhsv->bhdv', w, kc, vc)
        n = decay[..., None] * n + jnp.einsum('bhs,bhsd->bhd', w, kc)
        return (C, n, m_new), h

    init = (jnp.zeros((B, H, dk, dv), jnp.float32),
            jnp.zeros((B, H, dk), jnp.float32),
            jnp.zeros((B, H), jnp.float32))
    xs = (to_chunks(q), to_chunks(k), to_chunks(v), to_chunks(log_i), to_chunks(log_f))
    _, hs = lax.scan(step, init, xs)
    return jnp.moveaxis(hs, (0, 2), (1, 3)).reshape(B, S, H, dv)


def hybrid_mixer(h, w_in, conv_w, conv_b, gate_b, norm_g, w_out):
    B, S, _ = h.shape
    z = (h @ w_in).astype(jnp.float32)
    sizes = [SB_WIDTH, SB_WIDTH, SB_WIDTH, ML_QK_WIDTH, ML_QK_WIDTH,
             ML_V_WIDTH, ML_V_WIDTH, ML_HEADS, ML_HEADS]
    idx = np.cumsum(sizes)[:-1].tolist()
    q_sb, k_sb, v_sb, q_ml, k_ml, v_ml, o_ml, i_ml, f_ml = jnp.split(z, idx, axis=-1)

    y_sb = stick_breaking_attention(q_sb.reshape(B, S, SB_HEADS, SB_HEAD_DIM),
                                    k_sb.reshape(B, S, SB_HEADS, SB_HEAD_DIM),
                                    v_sb.reshape(B, S, SB_HEADS, SB_HEAD_DIM))
    y_sb = head_norm(y_sb, norm_g[:SB_WIDTH]).reshape(B, S, SB_WIDTH)

    qk = jax.nn.silu(causal_depthwise_conv(jnp.concatenate([q_ml, k_ml], -1), conv_w, conv_b))
    q_ml, k_ml = jnp.split(qk, 2, axis=-1)
    q_ml = q_ml * (ML_QK_DIM ** -0.5)
    log_i = i_ml + gate_b[:ML_HEADS]
    log_f = jax.nn.log_sigmoid(f_ml + gate_b[ML_HEADS:])
    y_ml = mlstm_chunkwise(q_ml.reshape(B, S, ML_HEADS, ML_QK_DIM),
                           k_ml.reshape(B, S, ML_HEADS, ML_QK_DIM),
                           v_ml.reshape(B, S, ML_HEADS, ML_V_DIM),
                           log_i, log_f)
    y_ml = jax.nn.sigmoid(o_ml) * head_norm(y_ml, norm_g[SB_WIDTH:]).reshape(B, S, ML_V_WIDTH)

    y = jnp.concatenate([y_sb, y_ml], axis=-1).astype(h.dtype)
    return y @ w_out


def setup_inputs(seed: int = 0) -> dict:
    key = jax.random.key(seed)
    ks = jax.random.split(key, 18)
    nrm = jax.random.normal
    f32 = jnp.float32
    x = nrm(ks[0], (BATCH, SEQ, D_MODEL), f32)
    c = nrm(ks[1], (BATCH, D_MODEL), f32)
    ada_w = nrm(ks[2], (DEPTH, D_MODEL, N_SUB * N_MOD * D_MODEL), f32) * (0.1 * D_MODEL ** -0.5)
    ada_b = nrm(ks[3], (DEPTH, N_SUB * N_MOD * D_MODEL), f32) * 0.01
    ln_g = 1.0 + 0.01 * nrm(ks[4], (DEPTH, N_SUB, D_MODEL), f32)
    ln_b = 0.01 * nrm(ks[5], (DEPTH, N_SUB, D_MODEL), f32)
    ffn1_w1 = nrm(ks[6], (DEPTH, D_MODEL, 2 * D_FF), f32) * D_MODEL ** -0.5
    ffn1_w2 = nrm(ks[7], (DEPTH, D_FF, D_MODEL), f32) * (BETA * D_FF ** -0.5)
    col_scale = jnp.concatenate([
        jnp.ones((2 * SB_WIDTH,), f32), jnp.full((SB_WIDTH,), BETA, f32),
        jnp.ones((2 * ML_QK_WIDTH,), f32), jnp.full((ML_V_WIDTH,), BETA, f32),
        jnp.ones((ML_V_WIDTH + 2 * ML_HEADS,), f32)])
    mix_w_in = nrm(ks[8], (DEPTH, D_MODEL, N_IN), f32) * (D_MODEL ** -0.5) * col_scale
    mlstm_conv_w = nrm(ks[9], (DEPTH, CONV_WIDTH, 2 * ML_QK_WIDTH), f32) * CONV_WIDTH ** -0.5
    mlstm_conv_b = 0.01 * nrm(ks[10], (DEPTH, 2 * ML_QK_WIDTH), f32)
    i_bias = 0.1 * nrm(ks[11], (DEPTH, ML_HEADS), f32)
    f_bias = jnp.linspace(3.0, 6.0, ML_HEADS, dtype=f32) + 0.1 * nrm(ks[12], (DEPTH, ML_HEADS), f32)
    mlstm_gate_b = jnp.concatenate([i_bias, f_bias], axis=-1)
    mix_norm_g = 1.0 + 0.01 * nrm(ks[13], (DEPTH, D_MIX), f32)
    mix_w_out = nrm(ks[14], (DEPTH, D_MIX, D_MODEL), f32) * (BETA * D_MIX ** -0.5)
    ffn2_w1 = nrm(ks[15], (DEPTH, D_MODEL, 2 * D_FF), f32) * D_MODEL ** -0.5
    ffn2_w2 = nrm(ks[16], (DEPTH, D_FF, D_MODEL), f32) * (BETA * D_FF ** -0.5)
    return {"x": x, "c": c, "ada_w": ada_w, "ada_b": ada_b, "ln_g": ln_g, "ln_b": ln_b,
            "ffn1_w1": ffn1_w1, "ffn1_w2": ffn1_w2, "mix_w_in": mix_w_in,
            "mlstm_conv_w": mlstm_conv_w, "mlstm_conv_b": mlstm_conv_b,
            "mlstm_gate_b": mlstm_gate_b, "mix_norm_g": mix_norm_g, "mix_w_out": mix_w_out,
            "ffn2_w1": ffn2_w1, "ffn2_w2": ffn2_w2}


def reference(x, c, ada_w, ada_b, ln_g, ln_b, ffn1_w1, ffn1_w2, mix_w_in,
              mlstm_conv_w, mlstm_conv_b, mlstm_gate_b, mix_norm_g, mix_w_out,
              ffn2_w1, ffn2_w2):
    B = x.shape[0]
    for l in range(DEPTH):
        mod = (jax.nn.silu(c) @ ada_w[l] + ada_b[l]).reshape(B, N_SUB, N_MOD, 1, D_MODEL)

        shift, scale, gate = mod[:, 0, 0], mod[:, 0, 1], mod[:, 0, 2]
        f = swiglu_ffn(x * (1 + scale) + shift, ffn1_w1[l], ffn1_w2[l])
        x = layer_norm(ALPHA * x + 0.5 * (1 + gate) * f, ln_g[l, 0], ln_b[l, 0])

        shift, scale, gate = mod[:, 1, 0], mod[:, 1, 1], mod[:, 1, 2]
        m = hybrid_mixer(x * (1 + scale) + shift, mix_w_in[l], mlstm_conv_w[l], mlstm_conv_b[l],
                         mlstm_gate_b[l], mix_norm_g[l], mix_w_out[l])
        x = layer_norm(ALPHA * x + (1 + gate) * m, ln_g[l, 1], ln_b[l, 1])

        shift, scale, gate = mod[:, 2, 0], mod[:, 2, 1], mod[:, 2, 2]
        f = swiglu_ffn(x * (1 + scale) + shift, ffn2_w1[l], ffn2_w2[l])
        x = layer_norm(ALPHA * x + 0.5 * (1 + gate) * f, ln_g[l, 2], ln_b[l, 2])
    return x
```

```python
import functools

import jax
import jax.numpy as jnp
from jax import lax
from jax.experimental import pallas as pl
from jax.experimental.pallas import tpu as pltpu

F32 = jnp.float32
BF16 = jnp.bfloat16

SB_HEADS = 8
SB_HEAD_DIM = 128
ML_HEADS = 4
ML_QK_DIM = 128
ML_V_DIM = 256
CONV_WIDTH = 4
N_SUB = 3
N_MOD = 3
LN_EPS = 1e-5
LANES = 128
SUBLANES = 8
NEG_BIG = -1e30

ADA_TN = 1024
FFN_TM = 512
FFN_TF = 512
PROJ_TM = 512
PROJ_TN = 1024
SB_BLK = 256
ML_CHUNK = 256
OUT_TM = 512
VMEM_LIMIT = 56 * 1024 * 1024


def _params(sem):
    return pltpu.CompilerParams(dimension_semantics=sem, vmem_limit_bytes=VMEM_LIMIT)


def _layer_norm_rows(y, g, b):
    mu = jnp.mean(y, axis=-1, keepdims=True)
    d = y - mu
    var = jnp.mean(d * d, axis=-1, keepdims=True)
    return d * lax.rsqrt(var + LN_EPS) * g + b


def _split_bf16(v):
    hi = v.astype(BF16)
    lo = (v - hi.astype(F32)).astype(BF16)
    return hi, lo


def _log_sigmoid(z):
    return jnp.minimum(z, 0.0) - jnp.log(1.0 + jnp.exp(-jnp.abs(z)))


def _adaln_kernel(c_ref, w_ref, b_ref, o_ref):
    cv = c_ref[...]
    sc = cv * jax.nn.sigmoid(cv)
    o_ref[0] = jnp.sum(w_ref[0] * sc, axis=0, keepdims=True) + b_ref[0]


def _adaln(c_col, ada_w, ada_b):
    depth, d, n = ada_w.shape
    return pl.pallas_call(
        _adaln_kernel,
        grid=(depth, n // ADA_TN),
        in_specs=[
            pl.BlockSpec((d, 1), lambda l, j: (0, 0)),
            pl.BlockSpec((1, d, ADA_TN), lambda l, j: (l, 0, j)),
            pl.BlockSpec((1, 1, ADA_TN), lambda l, j: (l, 0, j)),
        ],
        out_specs=pl.BlockSpec((1, 1, ADA_TN), lambda l, j: (l, 0, j)),
        out_shape=jax.ShapeDtypeStruct((depth, 1, n), F32),
        compiler_params=_params(("arbitrary", "arbitrary")),
        name="adaln_gemv",
    )(c_col, ada_w, ada_b.reshape(depth, 1, n))


def _ffn_kernel(x_ref, shift_ref, scale_ref, gate_ref, w1a_ref, w1u_ref, w2_ref,
                g_ref, b_ref, o_ref, h_ref, *, alpha, n_f):
    j = pl.program_id(1)

    @pl.when(j == 0)
    def _():
        h = x_ref[...] * (1.0 + scale_ref[...]) + shift_ref[...]
        h_ref[...] = h.astype(BF16)
        o_ref[...] = jnp.zeros_like(o_ref)

    h = h_ref[...]
    a = jnp.dot(h, w1a_ref[...], preferred_element_type=F32)
    u = jnp.dot(h, w1u_ref[...], preferred_element_type=F32)
    act = (a * jax.nn.sigmoid(a) * u).astype(BF16)
    o_ref[...] += jnp.dot(act, w2_ref[...], preferred_element_type=F32)

    @pl.when(j == n_f - 1)
    def _():
        y = alpha * x_ref[...] + (0.5 * (1.0 + gate_ref[...])) * o_ref[...]
        o_ref[...] = _layer_norm_rows(y, g_ref[...], b_ref[...])


def _ffn(x, shift, scale, gate, w1, w2, g, b, alpha):
    s, d = x.shape
    d_ff = w2.shape[0]
    n_f = d_ff // FFN_TF
    row = lambda i, j: (i, 0)
    vec = lambda i, j: (0, 0)
    return pl.pallas_call(
        functools.partial(_ffn_kernel, alpha=alpha, n_f=n_f),
        grid=(s // FFN_TM, n_f),
        in_specs=[
            pl.BlockSpec((FFN_TM, d), row),
            pl.BlockSpec((1, d), vec),
            pl.BlockSpec((1, d), vec),
            pl.BlockSpec((1, d), vec),
            pl.BlockSpec((d, FFN_TF), lambda i, j: (0, j)),
            pl.BlockSpec((d, FFN_TF), lambda i, j: (0, j + n_f)),
            pl.BlockSpec((FFN_TF, d), lambda i, j: (j, 0)),
            pl.BlockSpec((1, d), vec),
            pl.BlockSpec((1, d), vec),
        ],
        out_specs=pl.BlockSpec((FFN_TM, d), row),
        out_shape=jax.ShapeDtypeStruct((s, d), F32),
        scratch_shapes=[pltpu.VMEM((FFN_TM, d), BF16)],
        compiler_params=_params(("parallel", "arbitrary")),
        name="ffn_sublayer",
    )(x, shift, scale, gate, w1, w1, w2, g, b)


def _proj_kernel(x_ref, shift_ref, scale_ref, w_ref, cs_ref, wg_ref, wgt_ref,
                 z_ref, gcol_ref, grow_ref, h_ref):
    j = pl.program_id(1)

    @pl.when(j == 0)
    def _():
        h = (x_ref[...] * (1.0 + scale_ref[...]) + shift_ref[...]).astype(BF16)
        h_ref[...] = h
        gcol_ref[...] = jnp.dot(h, wg_ref[...], preferred_element_type=F32)
        grow_ref[...] = lax.dot_general(wgt_ref[...], h, (((1,), (1,)), ((), ())),
                                        preferred_element_type=F32)

    z = jnp.dot(h_ref[...], w_ref[...], preferred_element_type=F32)
    z_ref[...] = (z * cs_ref[...]).astype(BF16)


def _proj(x, shift, scale, w_main, col_scale, w_gate, w_gate_t):
    s, d = x.shape
    n = w_main.shape[1]
    gr = w_gate_t.shape[0]
    row = lambda i, j: (i, 0)
    vec = lambda i, j: (0, 0)
    return pl.pallas_call(
        _proj_kernel,
        grid=(s // PROJ_TM, n // PROJ_TN),
        in_specs=[
            pl.BlockSpec((PROJ_TM, d), row),
            pl.BlockSpec((1, d), vec),
            pl.BlockSpec((1, d), vec),
            pl.BlockSpec((d, PROJ_TN), lambda i, j: (0, j)),
            pl.BlockSpec((1, PROJ_TN), lambda i, j: (0, j)),
            pl.BlockSpec((d, LANES), vec),
            pl.BlockSpec((gr, d), vec),
        ],
        out_specs=[
            pl.BlockSpec((PROJ_TM, PROJ_TN), lambda i, j: (i, j)),
            pl.BlockSpec((PROJ_TM, LANES), row),
            pl.BlockSpec((gr, PROJ_TM), lambda i, j: (0, i)),
        ],
        out_shape=[
            jax.ShapeDtypeStruct((s, n), BF16),
            jax.ShapeDtypeStruct((s, LANES), F32),
            jax.ShapeDtypeStruct((gr, s), F32),
        ],
        scratch_shapes=[pltpu.VMEM((PROJ_TM, d), BF16)],
        compiler_params=_params(("parallel", "arbitrary")),
        name="mixer_in_proj",
    )(x, shift, scale, w_main, col_scale, w_gate, w_gate_t)


def _sb_kernel(q_ref, k_ref, v_ref, g_ref, o_ref):
    blk = SB_BLK
    i = pl.program_id(1)
    q = q_ref[...]
    r_id = lax.broadcasted_iota(jnp.int32, (blk, blk), 0)
    c_id = lax.broadcasted_iota(jnp.int32, (blk, blk), 1)
    later = (r_id > c_id).astype(BF16)
    causal = c_id < r_id

    def visit(j, carry, acc, diagonal):
        start = pl.multiple_of(j * blk, blk)
        ks = k_ref[pl.ds(start, blk), :]
        vs = v_ref[pl.ds(start, blk), :]
        z = lax.dot_general(q, ks, (((1,), (1,)), ((), ())), preferred_element_type=F32)
        if diagonal:
            z = jnp.where(causal, z, NEG_BIG)
        l1m = _log_sigmoid(-z)
        hi, lo = _split_bf16(l1m)
        after = (jnp.dot(hi, later, preferred_element_type=F32)
                 + jnp.dot(lo, later, preferred_element_type=F32))
        a = jnp.exp(z + l1m + after + carry)
        acc = acc + jnp.dot(a.astype(BF16), vs, preferred_element_type=F32)
        carry = carry + (after[:, 0:1] + l1m[:, 0:1])
        return carry, acc

    carry0 = jnp.zeros((blk, 1), F32)
    acc0 = jnp.zeros((blk, SB_HEAD_DIM), F32)
    carry, acc = visit(i, carry0, acc0, True)

    def body(n, state):
        return visit(i - 1 - n, state[0], state[1], False)

    carry, acc = lax.fori_loop(0, i, body, (carry, acc))

    mu = jnp.mean(acc, axis=-1, keepdims=True)
    d = acc - mu
    var = jnp.mean(d * d, axis=-1, keepdims=True)
    o_ref[...] = (d * lax.rsqrt(var + LN_EPS) * g_ref[...]).astype(o_ref.dtype)


def _sb_attention(z_main, norm_g_sb):
    s = z_main.shape[0]
    hd = SB_HEAD_DIM
    return pl.pallas_call(
        _sb_kernel,
        grid=(SB_HEADS, s // SB_BLK),
        in_specs=[
            pl.BlockSpec((SB_BLK, hd), lambda h, i: (i, h)),
            pl.BlockSpec((s, hd), lambda h, i: (0, SB_HEADS + h)),
            pl.BlockSpec((s, hd), lambda h, i: (0, 2 * SB_HEADS + h)),
            pl.BlockSpec((1, hd), lambda h, i: (0, h)),
        ],
        out_specs=pl.BlockSpec((SB_BLK, hd), lambda h, i: (i, h)),
        out_shape=jax.ShapeDtypeStruct((s, SB_HEADS * hd), BF16),
        compiler_params=_params(("parallel", "arbitrary")),
        name="stickbreak_attn",
    )(z_main, z_main, z_main, norm_g_sb)


def _mlstm_kernel(zq_ref, zk_ref, zv_ref, zo_ref, gcol_ref, grow_ref, cw_ref, cb_ref,
                  gb_row_ref, gb_col_ref, ng_ref, y_ref, ubuf_ref, c_ref, m_ref):
    L = ML_CHUNK
    nqk = ML_HEADS * ML_QK_DIM
    step = pl.program_id(0)

    @pl.when(step == 0)
    def _():
        ubuf_ref[0:SUBLANES, :] = jnp.zeros((SUBLANES, 2 * nqk), F32)
        c_ref[...] = jnp.zeros_like(c_ref)
        m_ref[...] = jnp.zeros_like(m_ref)

    ubuf_ref[SUBLANES:SUBLANES + L, 0:nqk] = zq_ref[...].astype(F32)
    ubuf_ref[SUBLANES:SUBLANES + L, nqk:2 * nqk] = zk_ref[...].astype(F32)
    conv = cb_ref[...]
    for tap in range(CONV_WIDTH):
        off = SUBLANES - (CONV_WIDTH - 1) + tap
        conv = conv + ubuf_ref[off:off + L, :] * cw_ref[tap:tap + 1, :]
    ubuf_ref[0:SUBLANES, :] = ubuf_ref[L:L + SUBLANES, :]
    qk = conv * jax.nn.sigmoid(conv)

    gc = gcol_ref[...] + gb_row_ref[...]
    gr = grow_ref[...] + gb_col_ref[...]
    r_id = lax.broadcasted_iota(jnp.int32, (L, L), 0)
    c_id = lax.broadcasted_iota(jnp.int32, (L, L), 1)
    causal = c_id <= r_id
    tri = causal.astype(BF16)
    tri_t = (r_id <= c_id).astype(BF16)
    lfc_hi, lfc_lo = _split_bf16(_log_sigmoid(gc))
    lfr_hi, lfr_lo = _split_bf16(_log_sigmoid(gr))
    bc = (jnp.dot(tri, lfc_hi, preferred_element_type=F32)
          + jnp.dot(tri, lfc_lo, preferred_element_type=F32))
    br = (jnp.dot(lfr_hi, tri_t, preferred_element_type=F32)
          + jnp.dot(lfr_lo, tri_t, preferred_element_type=F32))

    lane = lax.broadcasted_iota(jnp.int32, (L, LANES), 1)
    ones_col = (lane == 0).astype(BF16)

    for h in range(ML_HEADS):
        b_col = bc[:, ML_HEADS + h:ML_HEADS + h + 1]
        i_col = gc[:, h:h + 1]
        b_row = br[ML_HEADS + h:ML_HEADS + h + 1, :]
        i_row = gr[h:h + 1, :]
        m_prev = m_ref[h][0:1, 0:1]

        log_d = jnp.where(causal, b_col + (i_row - b_row), NEG_BIG)
        log_g = b_col + m_prev
        m_t = jnp.maximum(jnp.max(log_d, axis=-1, keepdims=True), log_g)
        p = jnp.exp(log_d - m_t)
        g = jnp.exp(log_g - m_t)

        qh = (qk[:, h * ML_QK_DIM:(h + 1) * ML_QK_DIM] * (ML_QK_DIM ** -0.5)).astype(BF16)
        kf = qk[:, nqk + h * ML_QK_DIM:nqk + (h + 1) * ML_QK_DIM]
        kh = kf.astype(BF16)
        vext = jnp.concatenate([zv_ref[:, h * ML_V_DIM:(h + 1) * ML_V_DIM], ones_col], axis=-1)

        sc = lax.dot_general(qh, kh, (((1,), (1,)), ((), ())), preferred_element_type=F32) * p
        state = c_ref[h]
        num = (jnp.dot(sc.astype(BF16), vext, preferred_element_type=F32)
               + g * jnp.dot(qh, state.astype(BF16), preferred_element_type=F32))
        den = num[:, ML_V_DIM:ML_V_DIM + 1]
        hv = num[:, 0:ML_V_DIM] * (1.0 / jnp.maximum(jnp.abs(den), jnp.exp(-m_t)))

        b_last = b_col[L - 1:L, :]
        log_w = b_last - b_col + i_col
        m_new = jnp.maximum(b_last + m_prev, jnp.max(log_w, axis=0, keepdims=True))
        decay = jnp.exp(b_last + m_prev - m_new)
        kw = (kf * jnp.exp(log_w - m_new)).astype(BF16)
        c_ref[h] = decay * state + lax.dot_general(
            kw, vext, (((0,), (0,)), ((), ())), preferred_element_type=F32)
        m_ref[h] = jnp.broadcast_to(m_new, (SUBLANES, LANES))

        mu = jnp.mean(hv, axis=-1, keepdims=True)
        d = hv - mu
        var = jnp.mean(d * d, axis=-1, keepdims=True)
        hn = d * lax.rsqrt(var + LN_EPS) * ng_ref[:, h * ML_V_DIM:(h + 1) * ML_V_DIM]
        og = jax.nn.sigmoid(zo_ref[:, h * ML_V_DIM:(h + 1) * ML_V_DIM].astype(F32))
        y_ref[:, h * ML_V_DIM:(h + 1) * ML_V_DIM] = (og * hn).astype(y_ref.dtype)


def _mlstm(z_main, gcol, grow, conv_w, conv_b, gb_row, gb_col, norm_g_ml):
    s = z_main.shape[0]
    L = ML_CHUNK
    nqk = ML_HEADS * ML_QK_DIM
    nv = ML_HEADS * ML_V_DIM
    sbw = SB_HEADS * SB_HEAD_DIM
    q_blk = (3 * sbw) // nqk
    v_blk = (3 * sbw + 2 * nqk) // nv
    gr = grow.shape[0]
    full = lambda c: (0, 0)
    return pl.pallas_call(
        _mlstm_kernel,
        grid=(s // L,),
        in_specs=[
            pl.BlockSpec((L, nqk), lambda c: (c, q_blk)),
            pl.BlockSpec((L, nqk), lambda c: (c, q_blk + 1)),
            pl.BlockSpec((L, nv), lambda c: (c, v_blk)),
            pl.BlockSpec((L, nv), lambda c: (c, v_blk + 1)),
            pl.BlockSpec((L, LANES), lambda c: (c, 0)),
            pl.BlockSpec((gr, L), lambda c: (0, c)),
            pl.BlockSpec((CONV_WIDTH, 2 * nqk), full),
            pl.BlockSpec((1, 2 * nqk), full),
            pl.BlockSpec((1, LANES), full),
            pl.BlockSpec((gr, 1), full),
            pl.BlockSpec((1, nv), full),
        ],
        out_specs=pl.BlockSpec((L, nv), lambda c: (c, 0)),
        out_shape=jax.ShapeDtypeStruct((s, nv), BF16),
        scratch_shapes=[
            pltpu.VMEM((L + SUBLANES, 2 * nqk), F32),
            pltpu.VMEM((ML_HEADS, ML_QK_DIM, ML_V_DIM + LANES), F32),
            pltpu.VMEM((ML_HEADS, SUBLANES, LANES), F32),
        ],
        compiler_params=_params(("arbitrary",)),
        name="mlstm_scan",
    )(z_main, z_main, z_main, z_main, gcol, grow, conv_w, conv_b, gb_row, gb_col, norm_g_ml)


def _outproj_kernel(x_ref, ysb_ref, yml_ref, wsb_ref, wml_ref, gate_ref, g_ref, b_ref, o_ref,
                    *, alpha):
    m = (jnp.dot(ysb_ref[...], wsb_ref[...], preferred_element_type=F32)
         + jnp.dot(yml_ref[...], wml_ref[...], preferred_element_type=F32))
    y = alpha * x_ref[...] + (1.0 + gate_ref[...]) * m
    o_ref[...] = _layer_norm_rows(y, g_ref[...], b_ref[...])


def _outproj(x, y_sb, y_ml, w_sb, w_ml, gate, g, b, alpha):
    s, d = x.shape
    row = lambda i: (i, 0)
    full = lambda i: (0, 0)
    return pl.pallas_call(
        functools.partial(_outproj_kernel, alpha=alpha),
        grid=(s // OUT_TM,),
        in_specs=[
            pl.BlockSpec((OUT_TM, d), row),
            pl.BlockSpec((OUT_TM, y_sb.shape[1]), row),
            pl.BlockSpec((OUT_TM, y_ml.shape[1]), row),
            pl.BlockSpec(w_sb.shape, full),
            pl.BlockSpec(w_ml.shape, full),
            pl.BlockSpec((1, d), full),
            pl.BlockSpec((1, d), full),
            pl.BlockSpec((1, d), full),
        ],
        out_specs=pl.BlockSpec((OUT_TM, d), row),
        out_shape=jax.ShapeDtypeStruct((s, d), F32),
        compiler_params=_params(("parallel",)),
        name="mixer_out_proj",
    )(x, y_sb, y_ml, w_sb, w_ml, gate, g, b)


def kernel(x, c, ada_w, ada_b, ln_g, ln_b, ffn1_w1, ffn1_w2, mix_w_in, mlstm_conv_w,
           mlstm_conv_b, mlstm_gate_b, mix_norm_g, mix_w_out, ffn2_w1, ffn2_w2):
    bsz, s, d = x.shape
    depth = ada_w.shape[0]
    assert bsz == 1, "kernels are written for a single sequence"
    assert s % max(FFN_TM, PROJ_TM, SB_BLK, ML_CHUNK, OUT_TM) == 0
    alpha = (2 * depth) ** 0.25
    sbw = SB_HEADS * SB_HEAD_DIM
    n_main = mix_w_in.shape[2] - 2 * ML_HEADS
    gate_rows = 2 * SUBLANES

    xs = x.reshape(s, d)
    mod = _adaln(c.reshape(d, 1), ada_w, ada_b).reshape(depth, N_SUB, N_MOD, 1, d)
    col_scale = jnp.concatenate(
        [jnp.full((1, sbw), SB_HEAD_DIM ** -0.5, F32), jnp.ones((1, n_main - sbw), F32)], axis=1)

    for l in range(depth):
        xs = _ffn(xs, mod[l, 0, 0], mod[l, 0, 1], mod[l, 0, 2],
                  ffn1_w1[l].astype(BF16), ffn1_w2[l].astype(BF16),
                  ln_g[l, 0][None], ln_b[l, 0][None], alpha)

        w_gate = mix_w_in[l][:, n_main:]
        w_gate_c = jnp.pad(w_gate, ((0, 0), (0, LANES - 2 * ML_HEADS))).astype(BF16)
        w_gate_t = jnp.pad(w_gate.T, ((0, gate_rows - 2 * ML_HEADS), (0, 0))).astype(BF16)
        z_main, gcol, grow = _proj(xs, mod[l, 1, 0], mod[l, 1, 1],
                                   mix_w_in[l][:, :n_main].astype(BF16), col_scale,
                                   w_gate_c, w_gate_t)
        gb = mlstm_gate_b[l]
        gb_row = jnp.pad(gb, (0, LANES - 2 * ML_HEADS))[None]
        gb_col = jnp.pad(gb, (0, gate_rows - 2 * ML_HEADS))[:, None]
        y_sb = _sb_attention(z_main, mix_norm_g[l][None, :sbw])
        y_ml = _mlstm(z_main, gcol, grow, mlstm_conv_w[l], mlstm_conv_b[l][None],
                      gb_row, gb_col, mix_norm_g[l][None, sbw:])
        w_out = mix_w_out[l].astype(BF16)
        xs = _outproj(xs, y_sb, y_ml, w_out[:sbw], w_out[sbw:], mod[l, 1, 2],
                      ln_g[l, 1][None], ln_b[l, 1][None], alpha)

        xs = _ffn(xs, mod[l, 2, 0], mod[l, 2, 1], mod[l, 2, 2],
                  ffn2_w1[l].astype(BF16), ffn2_w2[l].astype(BF16),
                  ln_g[l, 2][None], ln_b[l, 2][None], alpha)

    return xs.reshape(bsz, s, d)
```

```python
import functools

import jax
import jax.numpy as jnp
from jax import lax
from jax.experimental import pallas as pl
from jax.experimental.pallas import tpu as pltpu

F32 = jnp.float32
BF16 = jnp.bfloat16

SB_HEADS = 8
SB_HEAD_DIM = 128
ML_HEADS = 4
ML_QK_DIM = 128
ML_V_DIM = 256
CONV_WIDTH = 4
N_SUB = 3
N_MOD = 3
LN_EPS = 1e-5
LANES = 128
SUBLANES = 8
NEG_BIG = -1e30
LOG2E = 1.4426950408889634
F32_EXP2_UNDERFLOW = 150.0

ADA_TN = 1024
FFN_TM = 512
FFN_TF = 512
PROJ_TM = 512
PROJ_TN = 1024
SB_BLK = 256
ML_CHUNK = 256
OUT_TM = 512
VMEM_LIMIT = 56 * 1024 * 1024


def _params(sem):
    return pltpu.CompilerParams(dimension_semantics=sem, vmem_limit_bytes=VMEM_LIMIT)


def _layer_norm_rows(y, g, b):
    mu = jnp.mean(y, axis=-1, keepdims=True)
    d = y - mu
    var = jnp.mean(d * d, axis=-1, keepdims=True)
    return d * lax.rsqrt(var + LN_EPS) * g + b


def _split_bf16(v):
    hi = v.astype(BF16)
    lo = (v - hi.astype(F32)).astype(BF16)
    return hi, lo


def _log_sigmoid(z):
    return jnp.minimum(z, 0.0) - jnp.log(1.0 + jnp.exp(-jnp.abs(z)))


def _adaln_kernel(c_ref, w_ref, b_ref, o_ref):
    cv = c_ref[...]
    sc = cv * jax.nn.sigmoid(cv)
    o_ref[0] = jnp.sum(w_ref[0] * sc, axis=0, keepdims=True) + b_ref[0]


def _adaln(c_col, ada_w, ada_b):
    depth, d, n = ada_w.shape
    return pl.pallas_call(
        _adaln_kernel,
        grid=(depth, n // ADA_TN),
        in_specs=[
            pl.BlockSpec((d, 1), lambda l, j: (0, 0)),
            pl.BlockSpec((1, d, ADA_TN), lambda l, j: (l, 0, j)),
            pl.BlockSpec((1, 1, ADA_TN), lambda l, j: (l, 0, j)),
        ],
        out_specs=pl.BlockSpec((1, 1, ADA_TN), lambda l, j: (l, 0, j)),
        out_shape=jax.ShapeDtypeStruct((depth, 1, n), F32),
        compiler_params=_params(("arbitrary", "arbitrary")),
        name="adaln_gemv",
    )(c_col, ada_w, ada_b.reshape(depth, 1, n))


def _ffn_kernel(x_ref, shift_ref, scale_ref, gate_ref, w1a_ref, w1u_ref, w2_ref,
                g_ref, b_ref, o_ref, h_ref, *, alpha, n_f):
    j = pl.program_id(1)

    @pl.when(j == 0)
    def _():
        h = x_ref[...] * (1.0 + scale_ref[...]) + shift_ref[...]
        h_ref[...] = h.astype(BF16)
        o_ref[...] = jnp.zeros_like(o_ref)

    h = h_ref[...]
    a = jnp.dot(h, w1a_ref[...], preferred_element_type=F32)
    u = jnp.dot(h, w1u_ref[...], preferred_element_type=F32)
    act = (a * jax.nn.sigmoid(a) * u).astype(BF16)
    o_ref[...] += jnp.dot(act, w2_ref[...], preferred_element_type=F32)

    @pl.when(j == n_f - 1)
    def _():
        y = alpha * x_ref[...] + (0.5 * (1.0 + gate_ref[...])) * o_ref[...]
        o_ref[...] = _layer_norm_rows(y, g_ref[...], b_ref[...])


def _ffn(x, shift, scale, gate, w1, w2, layer, g, b, alpha):
    s, d = x.shape
    d_ff = w2.shape[1]
    n_f = d_ff // FFN_TF
    row = lambda i, j: (i, 0)
    vec = lambda i, j: (0, 0)
    return pl.pallas_call(
        functools.partial(_ffn_kernel, alpha=alpha, n_f=n_f),
        grid=(s // FFN_TM, n_f),
        in_specs=[
            pl.BlockSpec((FFN_TM, d), row),
            pl.BlockSpec((1, d), vec),
            pl.BlockSpec((1, d), vec),
            pl.BlockSpec((1, d), vec),
            pl.BlockSpec((None, d, FFN_TF), lambda i, j: (layer, 0, j)),
            pl.BlockSpec((None, d, FFN_TF), lambda i, j: (layer, 0, j + n_f)),
            pl.BlockSpec((None, FFN_TF, d), lambda i, j: (layer, j, 0)),
            pl.BlockSpec((1, d), vec),
            pl.BlockSpec((1, d), vec),
        ],
        out_specs=pl.BlockSpec((FFN_TM, d), row),
        out_shape=jax.ShapeDtypeStruct((s, d), F32),
        scratch_shapes=[pltpu.VMEM((FFN_TM, d), BF16)],
        compiler_params=_params(("parallel", "arbitrary")),
        name="ffn_sublayer",
    )(x, shift, scale, gate, w1, w1, w2, g, b)


def _proj_kernel(x_ref, shift_ref, scale_ref, w_ref, cs_ref, wg_ref, wgt_ref,
                 z_ref, gcol_ref, grow_ref, h_ref):
    j = pl.program_id(1)

    @pl.when(j == 0)
    def _():
        h = (x_ref[...] * (1.0 + scale_ref[...]) + shift_ref[...]).astype(BF16)
        h_ref[...] = h
        gcol_ref[...] = jnp.dot(h, wg_ref[...], preferred_element_type=F32)
        grow_ref[...] = lax.dot_general(wgt_ref[...], h, (((1,), (1,)), ((), ())),
                                        preferred_element_type=F32)

    z = jnp.dot(h_ref[...], w_ref[...], preferred_element_type=F32)
    z_ref[...] = (z * cs_ref[...]).astype(BF16)


def _proj(x, shift, scale, w_in, layer, col_scale, w_gate, w_gate_t):
    s, d = x.shape
    n = col_scale.shape[1]
    gr = w_gate_t.shape[0]
    row = lambda i, j: (i, 0)
    vec = lambda i, j: (0, 0)
    return pl.pallas_call(
        _proj_kernel,
        grid=(s // PROJ_TM, n // PROJ_TN),
        in_specs=[
            pl.BlockSpec((PROJ_TM, d), row),
            pl.BlockSpec((1, d), vec),
            pl.BlockSpec((1, d), vec),
            pl.BlockSpec((None, d, PROJ_TN), lambda i, j: (layer, 0, j)),
            pl.BlockSpec((1, PROJ_TN), lambda i, j: (0, j)),
            pl.BlockSpec((d, LANES), vec),
            pl.BlockSpec((gr, d), vec),
        ],
        out_specs=[
            pl.BlockSpec((PROJ_TM, PROJ_TN), lambda i, j: (i, j)),
            pl.BlockSpec((PROJ_TM, LANES), row),
            pl.BlockSpec((gr, PROJ_TM), lambda i, j: (0, i)),
        ],
        out_shape=[
            jax.ShapeDtypeStruct((s, n), BF16),
            jax.ShapeDtypeStruct((s, LANES), F32),
            jax.ShapeDtypeStruct((gr, s), F32),
        ],
        scratch_shapes=[pltpu.VMEM((PROJ_TM, d), BF16)],
        compiler_params=_params(("parallel", "arbitrary")),
        name="mixer_in_proj",
    )(x, shift, scale, w_in, col_scale, w_gate, w_gate_t)


def _sb_kernel(q_ref, k_ref, v_ref, g_ref, o_ref):
    blk = SB_BLK
    i = pl.program_id(1)
    q = q_ref[...]
    r_id = lax.broadcasted_iota(jnp.int32, (blk, blk), 0)
    c_id = lax.broadcasted_iota(jnp.int32, (blk, blk), 1)
    later = (r_id > c_id).astype(BF16)
    later2 = jnp.concatenate([later, later], axis=0)
    causal = c_id < r_id
    sign_bit = jnp.uint32(0x80000000)

    def visit(j, used, acc, diagonal):
        start = pl.multiple_of(j * blk, blk)
        ks = k_ref[pl.ds(start, blk), :]
        vs = v_ref[pl.ds(start, blk), :]
        z = lax.dot_general(q, ks, (((1,), (1,)), ((), ())), preferred_element_type=F32)
        if diagonal:
            z = jnp.where(causal, z, NEG_BIG)
        neg_abs = lax.bitcast_convert_type(lax.bitcast_convert_type(z, jnp.uint32) | sign_bit, F32)
        sp = jnp.maximum(z, 0.0) + jnp.log(1.0 + jnp.exp2(neg_abs)) * LOG2E
        hi, lo = _split_bf16(sp)
        after = jnp.dot(jnp.concatenate([hi, lo], axis=1), later2,
                        preferred_element_type=F32)
        a = jnp.exp2(z - sp - after - used)
        acc = acc + jnp.dot(a.astype(BF16), vs, preferred_element_type=F32)
        used = used + (after[:, 0:1] + sp[:, 0:1])
        return used, acc

    used0 = jnp.zeros((blk, 1), F32)
    acc0 = jnp.zeros((blk, SB_HEAD_DIM), F32)
    used, acc = visit(i, used0, acc0, True)

    def more(state):
        n, used, _ = state
        return jnp.logical_and(n < i, jnp.min(used) < F32_EXP2_UNDERFLOW)

    def body(state):
        n, used, acc = state
        used, acc = visit(i - 1 - n, used, acc, False)
        return n + 1, used, acc

    _, used, acc = lax.while_loop(more, body, (jnp.int32(0), used, acc))

    mu = jnp.mean(acc, axis=-1, keepdims=True)
    d = acc - mu
    var = jnp.mean(d * d, axis=-1, keepdims=True)
    o_ref[...] = (d * lax.rsqrt(var + LN_EPS) * g_ref[...]).astype(o_ref.dtype)


def _sb_attention(z_main, norm_g_sb):
    s = z_main.shape[0]
    hd = SB_HEAD_DIM
    return pl.pallas_call(
        _sb_kernel,
        grid=(SB_HEADS, s // SB_BLK),
        in_specs=[
            pl.BlockSpec((SB_BLK, hd), lambda h, i: (i, h)),
            pl.BlockSpec((s, hd), lambda h, i: (0, SB_HEADS + h)),
            pl.BlockSpec((s, hd), lambda h, i: (0, 2 * SB_HEADS + h)),
            pl.BlockSpec((1, hd), lambda h, i: (0, h)),
        ],
        out_specs=pl.BlockSpec((SB_BLK, hd), lambda h, i: (i, h)),
        out_shape=jax.ShapeDtypeStruct((s, SB_HEADS * hd), BF16),
        compiler_params=_params(("parallel", "arbitrary")),
        name="stickbreak_attn",
    )(z_main, z_main, z_main, norm_g_sb)


def _mlstm_kernel(zq_ref, zk_ref, zv_ref, zo_ref, gcol_ref, grow_ref, cw_ref, cb_ref,
                  gb_row_ref, gb_col_ref, ng_ref, y_ref, ubuf_ref, c_ref, m_ref):
    L = ML_CHUNK
    nqk = ML_HEADS * ML_QK_DIM
    step = pl.program_id(0)

    @pl.when(step == 0)
    def _():
        ubuf_ref[0:SUBLANES, :] = jnp.zeros((SUBLANES, 2 * nqk), F32)
        c_ref[...] = jnp.zeros_like(c_ref)
        m_ref[...] = jnp.zeros_like(m_ref)

    ubuf_ref[SUBLANES:SUBLANES + L, 0:nqk] = zq_ref[...].astype(F32)
    ubuf_ref[SUBLANES:SUBLANES + L, nqk:2 * nqk] = zk_ref[...].astype(F32)
    conv = cb_ref[...]
    for tap in range(CONV_WIDTH):
        off = SUBLANES - (CONV_WIDTH - 1) + tap
        conv = conv + ubuf_ref[off:off + L, :] * cw_ref[tap:tap + 1, :]
    ubuf_ref[0:SUBLANES, :] = ubuf_ref[L:L + SUBLANES, :]
    qk = conv * jax.nn.sigmoid(conv)

    gc = gcol_ref[...] + gb_row_ref[...]
    gr = grow_ref[...] + gb_col_ref[...]
    r_id = lax.broadcasted_iota(jnp.int32, (L, L), 0)
    c_id = lax.broadcasted_iota(jnp.int32, (L, L), 1)
    causal = c_id <= r_id
    tri = causal.astype(BF16)
    tri_t = (r_id <= c_id).astype(BF16)
    lfc_hi, lfc_lo = _split_bf16(_log_sigmoid(gc))
    lfr_hi, lfr_lo = _split_bf16(_log_sigmoid(gr))
    bc = (jnp.dot(tri, lfc_hi, preferred_element_type=F32)
          + jnp.dot(tri, lfc_lo, preferred_element_type=F32))
    br = (jnp.dot(lfr_hi, tri_t, preferred_element_type=F32)
          + jnp.dot(lfr_lo, tri_t, preferred_element_type=F32))

    lane = lax.broadcasted_iota(jnp.int32, (L, LANES), 1)
    ones_col = (lane == 0).astype(BF16)

    for h in range(ML_HEADS):
        b_col = bc[:, ML_HEADS + h:ML_HEADS + h + 1]
        i_col = gc[:, h:h + 1]
        b_row = br[ML_HEADS + h:ML_HEADS + h + 1, :]
        i_row = gr[h:h + 1, :]
        m_prev = m_ref[h][0:1, 0:1]

        log_d = jnp.where(causal, b_col + (i_row - b_row), NEG_BIG)
        log_g = b_col + m_prev
        m_t = jnp.maximum(jnp.max(log_d, axis=-1, keepdims=True), log_g)
        p = jnp.exp(log_d - m_t)
        g = jnp.exp(log_g - m_t)

        qh = (qk[:, h * ML_QK_DIM:(h + 1) * ML_QK_DIM] * (ML_QK_DIM ** -0.5)).astype(BF16)
        kf = qk[:, nqk + h * ML_QK_DIM:nqk + (h + 1) * ML_QK_DIM]
        kh = kf.astype(BF16)
        vext = jnp.concatenate([zv_ref[:, h * ML_V_DIM:(h + 1) * ML_V_DIM], ones_col], axis=-1)

        sc = lax.dot_general(qh, kh, (((1,), (1,)), ((), ())), preferred_element_type=F32) * p
        state = c_ref[h]
        num = (jnp.dot(sc.astype(BF16), vext, preferred_element_type=F32)
               + g * jnp.dot(qh, state.astype(BF16), preferred_element_type=F32))
        den = num[:, ML_V_DIM:ML_V_DIM + 1]
        hv = num[:, 0:ML_V_DIM] * (1.0 / jnp.maximum(jnp.abs(den), jnp.exp(-m_t)))

        b_last = b_col[L - 1:L, :]
        log_w = b_last - b_col + i_col
        m_new = jnp.maximum(b_last + m_prev, jnp.max(log_w, axis=0, keepdims=True))
        decay = jnp.exp(b_last + m_prev - m_new)
        kw = (kf * jnp.exp(log_w - m_new)).astype(BF16)
        c_ref[h] = decay * state + lax.dot_general(
            kw, vext, (((0,), (0,)), ((), ())), preferred_element_type=F32)
        m_ref[h] = jnp.broadcast_to(m_new, (SUBLANES, LANES))

        mu = jnp.mean(hv, axis=-1, keepdims=True)
        d = hv - mu
        var = jnp.mean(d * d, axis=-1, keepdims=True)
        hn = d * lax.rsqrt(var + LN_EPS) * ng_ref[:, h * ML_V_DIM:(h + 1) * ML_V_DIM]
        og = jax.nn.sigmoid(zo_ref[:, h * ML_V_DIM:(h + 1) * ML_V_DIM].astype(F32))
        y_ref[:, h * ML_V_DIM:(h + 1) * ML_V_DIM] = (og * hn).astype(y_ref.dtype)


def _mlstm(z_main, gcol, grow, conv_w, conv_b, gb_row, gb_col, norm_g_ml):
    s = z_main.shape[0]
    L = ML_CHUNK
    nqk = ML_HEADS * ML_QK_DIM
    nv = ML_HEADS * ML_V_DIM
    sbw = SB_HEADS * SB_HEAD_DIM
    q_blk = (3 * sbw) // nqk
    v_blk = (3 * sbw + 2 * nqk) // nv
    gr = grow.shape[0]
    full = lambda c: (0, 0)
    return pl.pallas_call(
        _mlstm_kernel,
        grid=(s // L,),
        in_specs=[
            pl.BlockSpec((L, nqk), lambda c: (c, q_blk)),
            pl.BlockSpec((L, nqk), lambda c: (c, q_blk + 1)),
            pl.BlockSpec((L, nv), lambda c: (c, v_blk)),
            pl.BlockSpec((L, nv), lambda c: (c, v_blk + 1)),
            pl.BlockSpec((L, LANES), lambda c: (c, 0)),
            pl.BlockSpec((gr, L), lambda c: (0, c)),
            pl.BlockSpec((CONV_WIDTH, 2 * nqk), full),
            pl.BlockSpec((1, 2 * nqk), full),
            pl.BlockSpec((1, LANES), full),
            pl.BlockSpec((gr, 1), full),
            pl.BlockSpec((1, nv), full),
        ],
        out_specs=pl.BlockSpec((L, nv), lambda c: (c, 0)),
        out_shape=jax.ShapeDtypeStruct((s, nv), BF16),
        scratch_shapes=[
            pltpu.VMEM((L + SUBLANES, 2 * nqk), F32),
            pltpu.VMEM((ML_HEADS, ML_QK_DIM, ML_V_DIM + LANES), F32),
            pltpu.VMEM((ML_HEADS, SUBLANES, LANES), F32),
        ],
        compiler_params=_params(("arbitrary",)),
        name="mlstm_scan",
    )(z_main, z_main, z_main, z_main, gcol, grow, conv_w, conv_b, gb_row, gb_col, norm_g_ml)


def _outproj_kernel(x_ref, ysb_ref, yml_ref, wsb_ref, wml_ref, gate_ref, g_ref, b_ref, o_ref,
                    *, alpha):
    m = (jnp.dot(ysb_ref[...], wsb_ref[...], preferred_element_type=F32)
         + jnp.dot(yml_ref[...], wml_ref[...], preferred_element_type=F32))
    y = alpha * x_ref[...] + (1.0 + gate_ref[...]) * m
    o_ref[...] = _layer_norm_rows(y, g_ref[...], b_ref[...])


def _outproj(x, y_sb, y_ml, w_out, layer, gate, g, b, alpha):
    s, d = x.shape
    sbw = y_sb.shape[1]
    assert y_ml.shape[1] == sbw and w_out.shape[1] == 2 * sbw
    row = lambda i: (i, 0)
    full = lambda i: (0, 0)
    return pl.pallas_call(
        functools.partial(_outproj_kernel, alpha=alpha),
        grid=(s // OUT_TM,),
        in_specs=[
            pl.BlockSpec((OUT_TM, d), row),
            pl.BlockSpec((OUT_TM, y_sb.shape[1]), row),
            pl.BlockSpec((OUT_TM, y_ml.shape[1]), row),
            pl.BlockSpec((None, sbw, d), lambda i: (layer, 0, 0)),
            pl.BlockSpec((None, sbw, d), lambda i: (layer, 1, 0)),
            pl.BlockSpec((1, d), full),
            pl.BlockSpec((1, d), full),
            pl.BlockSpec((1, d), full),
        ],
        out_specs=pl.BlockSpec((OUT_TM, d), row),
        out_shape=jax.ShapeDtypeStruct((s, d), F32),
        compiler_params=_params(("parallel",)),
        name="mixer_out_proj",
    )(x, y_sb, y_ml, w_out, w_out, gate, g, b)


def kernel(x, c, ada_w, ada_b, ln_g, ln_b, ffn1_w1, ffn1_w2, mix_w_in, mlstm_conv_w,
           mlstm_conv_b, mlstm_gate_b, mix_norm_g, mix_w_out, ffn2_w1, ffn2_w2):
    bsz, s, d = x.shape
    depth = ada_w.shape[0]
    assert bsz == 1, "kernels are written for a single sequence"
    assert s % max(FFN_TM, PROJ_TM, SB_BLK, ML_CHUNK, OUT_TM) == 0
    alpha = (2 * depth) ** 0.25
    sbw = SB_HEADS * SB_HEAD_DIM
    n_main = mix_w_in.shape[2] - 2 * ML_HEADS
    gate_rows = 2 * SUBLANES

    xs = x.reshape(s, d)
    mod = _adaln(c.reshape(d, 1), ada_w, ada_b).reshape(depth, N_SUB, N_MOD, 1, d)
    col_scale = jnp.concatenate(
        [jnp.full((1, sbw), SB_HEAD_DIM ** -0.5 * LOG2E, F32), jnp.ones((1, n_main - sbw), F32)],
        axis=1)

    f1w1, f1w2 = ffn1_w1.astype(BF16), ffn1_w2.astype(BF16)
    f2w1, f2w2 = ffn2_w1.astype(BF16), ffn2_w2.astype(BF16)
    w_in, w_out = mix_w_in.astype(BF16), mix_w_out.astype(BF16)

    for l in range(depth):
        xs = _ffn(xs, mod[l, 0, 0], mod[l, 0, 1], mod[l, 0, 2], f1w1, f1w2, l,
                  ln_g[l, 0][None], ln_b[l, 0][None], alpha)

        w_gate = w_in[l, :, n_main:]
        w_gate_c = jnp.pad(w_gate, ((0, 0), (0, LANES - 2 * ML_HEADS)))
        w_gate_t = jnp.pad(w_gate.T, ((0, gate_rows - 2 * ML_HEADS), (0, 0)))
        z_main, gcol, grow = _proj(xs, mod[l, 1, 0], mod[l, 1, 1], w_in, l, col_scale,
                                   w_gate_c, w_gate_t)
        gb = mlstm_gate_b[l]
        gb_row = jnp.pad(gb, (0, LANES - 2 * ML_HEADS))[None]
        gb_col = jnp.pad(gb, (0, gate_rows - 2 * ML_HEADS))[:, None]
        y_sb = _sb_attention(z_main, mix_norm_g[l][None, :sbw])
        y_ml = _mlstm(z_main, gcol, grow, mlstm_conv_w[l], mlstm_conv_b[l][None],
                      gb_row, gb_col, mix_norm_g[l][None, sbw:])
        xs = _outproj(xs, y_sb, y_ml, w_out, l, mod[l, 1, 2],
                      ln_g[l, 1][None], ln_b[l, 1][None], alpha)

        xs = _ffn(xs, mod[l, 2, 0], mod[l, 2, 1], mod[l, 2, 2], f2w1, f2w2, l,
                  ln_g[l, 2][None], ln_b[l, 2][None], alpha)

    return xs.reshape(bsz, s, d)
```

```python
import functools

import jax
import jax.numpy as jnp
from jax import lax
from jax.experimental import pallas as pl
from jax.experimental.pallas import tpu as pltpu

F32 = jnp.float32
BF16 = jnp.bfloat16

SB_HEADS = 8
SB_HEAD_DIM = 128
ML_HEADS = 4
ML_QK_DIM = 128
ML_V_DIM = 256
CONV_WIDTH = 4
N_SUB = 3
N_MOD = 3
LN_EPS = 1e-5
LANES = 128
SUBLANES = 8
NEG_BIG = -1e30
LOG2E = 1.4426950408889634
F32_EXP2_UNDERFLOW = 150.0

ADA_TN = 1024
FFN_TM = 512
FFN_TF = 512
PROJ_TM = 512
PROJ_TN = 1024
SB_BLK = 256
SB_PACK = 2
ML_CHUNK = 256
OUT_TM = 512
VMEM_LIMIT = 56 * 1024 * 1024


def _params(sem):
    return pltpu.CompilerParams(dimension_semantics=sem, vmem_limit_bytes=VMEM_LIMIT)


def _layer_norm_rows(y, g, b):
    mu = jnp.mean(y, axis=-1, keepdims=True)
    d = y - mu
    var = jnp.mean(d * d, axis=-1, keepdims=True)
    return d * lax.rsqrt(var + LN_EPS) * g + b


def _split_bf16(v):
    hi = v.astype(BF16)
    lo = (v - hi.astype(F32)).astype(BF16)
    return hi, lo


def _log_sigmoid(z):
    return jnp.minimum(z, 0.0) - jnp.log(1.0 + jnp.exp(-jnp.abs(z)))


def _adaln_kernel(c_ref, w_ref, b_ref, o_ref):
    cv = c_ref[...]
    sc = cv * jax.nn.sigmoid(cv)
    o_ref[0] = jnp.sum(w_ref[0] * sc, axis=0, keepdims=True) + b_ref[0]


def _adaln(c_col, ada_w, ada_b):
    depth, d, n = ada_w.shape
    return pl.pallas_call(
        _adaln_kernel,
        grid=(depth, n // ADA_TN),
        in_specs=[
            pl.BlockSpec((d, 1), lambda l, j: (0, 0)),
            pl.BlockSpec((1, d, ADA_TN), lambda l, j: (l, 0, j)),
            pl.BlockSpec((1, 1, ADA_TN), lambda l, j: (l, 0, j)),
        ],
        out_specs=pl.BlockSpec((1, 1, ADA_TN), lambda l, j: (l, 0, j)),
        out_shape=jax.ShapeDtypeStruct((depth, 1, n), F32),
        compiler_params=_params(("arbitrary", "arbitrary")),
        name="adaln_gemv",
    )(c_col, ada_w, ada_b.reshape(depth, 1, n))


def _ffn_kernel(x_ref, shift_ref, scale_ref, gate_ref, w1a_ref, w1u_ref, w2_ref,
                g_ref, b_ref, o_ref, h_ref, *, alpha, n_f):
    j = pl.program_id(1)

    @pl.when(j == 0)
    def _():
        h = x_ref[...] * (1.0 + scale_ref[...]) + shift_ref[...]
        h_ref[...] = h.astype(BF16)
        o_ref[...] = jnp.zeros_like(o_ref)

    h = h_ref[...]
    a = jnp.dot(h, w1a_ref[...], preferred_element_type=F32)
    u = jnp.dot(h, w1u_ref[...], preferred_element_type=F32)
    act = (a * jax.nn.sigmoid(a) * u).astype(BF16)
    o_ref[...] += jnp.dot(act, w2_ref[...], preferred_element_type=F32)

    @pl.when(j == n_f - 1)
    def _():
        y = alpha * x_ref[...] + (0.5 * (1.0 + gate_ref[...])) * o_ref[...]
        o_ref[...] = _layer_norm_rows(y, g_ref[...], b_ref[...])


def _ffn(x, shift, scale, gate, w1, w2, layer, g, b, alpha):
    s, d = x.shape
    d_ff = w2.shape[1]
    n_f = d_ff // FFN_TF
    row = lambda i, j: (i, 0)
    vec = lambda i, j: (0, 0)
    return pl.pallas_call(
        functools.partial(_ffn_kernel, alpha=alpha, n_f=n_f),
        grid=(s // FFN_TM, n_f),
        in_specs=[
            pl.BlockSpec((FFN_TM, d), row),
            pl.BlockSpec((1, d), vec),
            pl.BlockSpec((1, d), vec),
            pl.BlockSpec((1, d), vec),
            pl.BlockSpec((None, d, FFN_TF), lambda i, j: (layer, 0, j)),
            pl.BlockSpec((None, d, FFN_TF), lambda i, j: (layer, 0, j + n_f)),
            pl.BlockSpec((None, FFN_TF, d), lambda i, j: (layer, j, 0)),
            pl.BlockSpec((1, d), vec),
            pl.BlockSpec((1, d), vec),
        ],
        out_specs=pl.BlockSpec((FFN_TM, d), row),
        out_shape=jax.ShapeDtypeStruct((s, d), F32),
        scratch_shapes=[pltpu.VMEM((FFN_TM, d), BF16)],
        compiler_params=_params(("parallel", "arbitrary")),
        name="ffn_sublayer",
    )(x, shift, scale, gate, w1, w1, w2, g, b)


def _proj_kernel(x_ref, shift_ref, scale_ref, w_ref, cs_ref, wg_ref, wgt_ref,
                 z_ref, gcol_ref, grow_ref, h_ref):
    j = pl.program_id(1)

    @pl.when(j == 0)
    def _():
        h = (x_ref[...] * (1.0 + scale_ref[...]) + shift_ref[...]).astype(BF16)
        h_ref[...] = h
        gcol_ref[...] = jnp.dot(h, wg_ref[...], preferred_element_type=F32)
        grow_ref[...] = lax.dot_general(wgt_ref[...], h, (((1,), (1,)), ((), ())),
                                        preferred_element_type=F32)

    z = jnp.dot(h_ref[...], w_ref[...], preferred_element_type=F32)
    z_ref[...] = (z * cs_ref[...]).astype(BF16)


def _proj(x, shift, scale, w_in, layer, col_scale, w_gate, w_gate_t):
    s, d = x.shape
    n = col_scale.shape[1]
    gr = w_gate_t.shape[0]
    row = lambda i, j: (i, 0)
    vec = lambda i, j: (0, 0)
    return pl.pallas_call(
        _proj_kernel,
        grid=(s // PROJ_TM, n // PROJ_TN),
        in_specs=[
            pl.BlockSpec((PROJ_TM, d), row),
            pl.BlockSpec((1, d), vec),
            pl.BlockSpec((1, d), vec),
            pl.BlockSpec((None, d, PROJ_TN), lambda i, j: (layer, 0, j)),
            pl.BlockSpec((1, PROJ_TN), lambda i, j: (0, j)),
            pl.BlockSpec((d, LANES), vec),
            pl.BlockSpec((gr, d), vec),
        ],
        out_specs=[
            pl.BlockSpec((PROJ_TM, PROJ_TN), lambda i, j: (i, j)),
            pl.BlockSpec((PROJ_TM, LANES), row),
            pl.BlockSpec((gr, PROJ_TM), lambda i, j: (0, i)),
        ],
        out_shape=[
            jax.ShapeDtypeStruct((s, n), BF16),
            jax.ShapeDtypeStruct((s, LANES), F32),
            jax.ShapeDtypeStruct((gr, s), F32),
        ],
        scratch_shapes=[pltpu.VMEM((PROJ_TM, d), BF16)],
        compiler_params=_params(("parallel", "arbitrary")),
        name="mixer_in_proj",
    )(x, shift, scale, w_in, col_scale, w_gate, w_gate_t)


def _sb_kernel(q_ref, k_ref, v_ref, g_ref, o_ref):
    blk = SB_BLK
    hd = SB_HEAD_DIM
    i = pl.program_id(1)
    lanes = [slice(p * hd, (p + 1) * hd) for p in range(SB_PACK)]
    qs = [q_ref[:, ln] for ln in lanes]
    r_id = lax.broadcasted_iota(jnp.int32, (blk, blk), 0)
    c_id = lax.broadcasted_iota(jnp.int32, (blk, blk), 1)
    later = (r_id > c_id).astype(BF16)
    causal = c_id < r_id

    def visit(j, used, acc, diagonal):
        start = pl.multiple_of(j * blk, blk)
        new_used, new_acc = [], []
        for p, ln in enumerate(lanes):
            ks = k_ref[pl.ds(start, blk), ln]
            vs = v_ref[pl.ds(start, blk), ln]
            z = lax.dot_general(qs[p], ks, (((1,), (1,)), ((), ())), preferred_element_type=F32)
            if diagonal:
                z = jnp.where(causal, z, NEG_BIG)
            sp = jnp.maximum(z, 0.0) + jnp.log(1.0 + jnp.exp2(-jnp.abs(z))) * LOG2E
            after = jnp.dot(sp.astype(BF16), later,
                            preferred_element_type=F32)
            a = jnp.exp2(z - sp - after - used[p])
            new_acc.append(acc[p] + jnp.dot(a.astype(BF16), vs, preferred_element_type=F32))
            new_used.append(used[p] + (after[:, 0:1] + sp[:, 0:1]))
        return tuple(new_used), tuple(new_acc)

    def first_pair():
        r2 = lax.broadcasted_iota(jnp.int32, (blk, 2 * blk), 0)
        c2 = lax.broadcasted_iota(jnp.int32, (blk, 2 * blk), 1)
        allowed = c2 < r2 + blk
        start = pl.multiple_of((i - 1) * blk, blk)
        new_used, new_acc = [], []
        for p, ln in enumerate(lanes):
            kw = k_ref[pl.ds(start, 2 * blk), ln]
            vw = v_ref[pl.ds(start, 2 * blk), ln]
            z = lax.dot_general(qs[p], kw, (((1,), (1,)), ((), ())), preferred_element_type=F32)
            z = jnp.where(allowed, z, NEG_BIG)
            sp = jnp.maximum(z, 0.0) + jnp.log(1.0 + jnp.exp2(-jnp.abs(z))) * LOG2E
            sp_b = sp.astype(BF16)
            after_hi = jnp.dot(sp_b[:, blk:], later, preferred_element_type=F32)
            total_hi = after_hi[:, 0:1] + sp[:, blk:blk + 1]
            after_lo = jnp.dot(sp_b[:, :blk], later, preferred_element_type=F32) + total_hi
            after = jnp.concatenate([after_lo, after_hi], axis=1)
            a = jnp.exp2(z - sp - after)
            new_acc.append(jnp.dot(a.astype(BF16), vw, preferred_element_type=F32))
            new_used.append(after_lo[:, 0:1] + sp[:, 0:1])
        return tuple(new_used), tuple(new_acc)

    def finish(acc):
        for p, ln in enumerate(lanes):
            mu = jnp.mean(acc[p], axis=-1, keepdims=True)
            d = acc[p] - mu
            var = jnp.mean(d * d, axis=-1, keepdims=True)
            o_ref[:, ln] = (d * lax.rsqrt(var + LN_EPS) * g_ref[:, ln]).astype(o_ref.dtype)

    @pl.when(i == 0)
    def _():
        used0 = tuple(jnp.zeros((blk, 1), F32) for _ in lanes)
        acc0 = tuple(jnp.zeros((blk, hd), F32) for _ in lanes)
        _, acc = visit(0, used0, acc0, True)
        finish(acc)

    @pl.when(i > 0)
    def _():
        used, acc = first_pair()

        def more(state):
            n, used, _ = state
            least = functools.reduce(jnp.minimum, used)
            return jnp.logical_and(n < i - 1, jnp.min(least) < F32_EXP2_UNDERFLOW)

        def body(state):
            n, used, acc = state
            used, acc = visit(i - 2 - n, used, acc, False)
            return n + 1, used, acc

        _, _, acc = lax.while_loop(more, body, (jnp.int32(0), used, acc))
        finish(acc)


def _sb_attention(z_main, norm_g_sb):
    s = z_main.shape[0]
    w = SB_PACK * SB_HEAD_DIM
    groups = SB_HEADS // SB_PACK
    return pl.pallas_call(
        _sb_kernel,
        grid=(groups, s // SB_BLK),
        in_specs=[
            pl.BlockSpec((SB_BLK, w), lambda h, i: (i, h)),
            pl.BlockSpec((s, w), lambda h, i: (0, groups + h)),
            pl.BlockSpec((s, w), lambda h, i: (0, 2 * groups + h)),
            pl.BlockSpec((1, w), lambda h, i: (0, h)),
        ],
        out_specs=pl.BlockSpec((SB_BLK, w), lambda h, i: (i, h)),
        out_shape=jax.ShapeDtypeStruct((s, SB_HEADS * SB_HEAD_DIM), BF16),
        compiler_params=_params(("parallel", "arbitrary")),
        name="stickbreak_attn",
    )(z_main, z_main, z_main, norm_g_sb)


def _mlstm_kernel(zq_ref, zk_ref, zv_ref, zo_ref, gcol_ref, grow_ref, cw_ref, cb_ref,
                  gb_row_ref, gb_col_ref, ng_ref, y_ref, ubuf_ref, c_ref, m_ref):
    L = ML_CHUNK
    nqk = ML_HEADS * ML_QK_DIM
    step = pl.program_id(0)

    @pl.when(step == 0)
    def _():
        ubuf_ref[0:SUBLANES, :] = jnp.zeros((SUBLANES, 2 * nqk), F32)
        c_ref[...] = jnp.zeros_like(c_ref)
        m_ref[...] = jnp.zeros_like(m_ref)

    ubuf_ref[SUBLANES:SUBLANES + L, 0:nqk] = zq_ref[...].astype(F32)
    ubuf_ref[SUBLANES:SUBLANES + L, nqk:2 * nqk] = zk_ref[...].astype(F32)
    conv = cb_ref[...]
    for tap in range(CONV_WIDTH):
        off = SUBLANES - (CONV_WIDTH - 1) + tap
        conv = conv + ubuf_ref[off:off + L, :] * cw_ref[tap:tap + 1, :]
    ubuf_ref[0:SUBLANES, :] = ubuf_ref[L:L + SUBLANES, :]
    qk = conv * jax.nn.sigmoid(conv)

    gc = gcol_ref[...] + gb_row_ref[...]
    gr = grow_ref[...] + gb_col_ref[...]
    r_id = lax.broadcasted_iota(jnp.int32, (L, L), 0)
    c_id = lax.broadcasted_iota(jnp.int32, (L, L), 1)
    causal = c_id <= r_id
    tri = causal.astype(BF16)
    tri_t = (r_id <= c_id).astype(BF16)
    lfc_hi, lfc_lo = _split_bf16(_log_sigmoid(gc))
    lfr_hi, lfr_lo = _split_bf16(_log_sigmoid(gr))
    bc = (jnp.dot(tri, lfc_hi, preferred_element_type=F32)
          + jnp.dot(tri, lfc_lo, preferred_element_type=F32))
    br = (jnp.dot(lfr_hi, tri_t, preferred_element_type=F32)
          + jnp.dot(lfr_lo, tri_t, preferred_element_type=F32))

    lane = lax.broadcasted_iota(jnp.int32, (L, LANES), 1)
    ones_col = (lane == 0).astype(BF16)

    for h in range(ML_HEADS):
        b_col = bc[:, ML_HEADS + h:ML_HEADS + h + 1]
        i_col = gc[:, h:h + 1]
        b_row = br[ML_HEADS + h:ML_HEADS + h + 1, :]
        i_row = gr[h:h + 1, :]
        m_prev = m_ref[h][0:1, 0:1]

        log_d = jnp.where(causal, b_col + (i_row - b_row), NEG_BIG)
        log_g = b_col + m_prev
        m_t = jnp.maximum(jnp.max(log_d, axis=-1, keepdims=True), log_g)
        p = jnp.exp(log_d - m_t)
        g = jnp.exp(log_g - m_t)

        qh = (qk[:, h * ML_QK_DIM:(h + 1) * ML_QK_DIM] * (ML_QK_DIM ** -0.5)).astype(BF16)
        kf = qk[:, nqk + h * ML_QK_DIM:nqk + (h + 1) * ML_QK_DIM]
        kh = kf.astype(BF16)
        vext = jnp.concatenate([zv_ref[:, h * ML_V_DIM:(h + 1) * ML_V_DIM], ones_col], axis=-1)

        sc = lax.dot_general(qh, kh, (((1,), (1,)), ((), ())), preferred_element_type=F32) * p
        state = c_ref[h]
        num = (jnp.dot(sc.astype(BF16), vext, preferred_element_type=F32)
               + g * jnp.dot(qh, state.astype(BF16), preferred_element_type=F32))
        den = num[:, ML_V_DIM:ML_V_DIM + 1]
        hv = num[:, 0:ML_V_DIM] * (1.0 / jnp.maximum(jnp.abs(den), jnp.exp(-m_t)))

        b_last = b_col[L - 1:L, :]
        log_w = b_last - b_col + i_col
        m_new = jnp.maximum(b_last + m_prev, jnp.max(log_w, axis=0, keepdims=True))
        decay = jnp.exp(b_last + m_prev - m_new)
        kw = (kf * jnp.exp(log_w - m_new)).astype(BF16)
        c_ref[h] = decay * state + lax.dot_general(
            kw, vext, (((0,), (0,)), ((), ())), preferred_element_type=F32)
        m_ref[h] = jnp.broadcast_to(m_new, (SUBLANES, LANES))

        mu = jnp.mean(hv, axis=-1, keepdims=True)
        d = hv - mu
        var = jnp.mean(d * d, axis=-1, keepdims=True)
        hn = d * lax.rsqrt(var + LN_EPS) * ng_ref[:, h * ML_V_DIM:(h + 1) * ML_V_DIM]
        og = jax.nn.sigmoid(zo_ref[:, h * ML_V_DIM:(h + 1) * ML_V_DIM].astype(F32))
        y_ref[:, h * ML_V_DIM:(h + 1) * ML_V_DIM] = (og * hn).astype(y_ref.dtype)


def _mlstm(z_main, gcol, grow, conv_w, conv_b, gb_row, gb_col, norm_g_ml):
    s = z_main.shape[0]
    L = ML_CHUNK
    nqk = ML_HEADS * ML_QK_DIM
    nv = ML_HEADS * ML_V_DIM
    sbw = SB_HEADS * SB_HEAD_DIM
    q_blk = (3 * sbw) // nqk
    v_blk = (3 * sbw + 2 * nqk) // nv
    gr = grow.shape[0]
    full = lambda c: (0, 0)
    return pl.pallas_call(
        _mlstm_kernel,
        grid=(s // L,),
        in_specs=[
            pl.BlockSpec((L, nqk), lambda c: (c, q_blk)),
            pl.BlockSpec((L, nqk), lambda c: (c, q_blk + 1)),
            pl.BlockSpec((L, nv), lambda c: (c, v_blk)),
            pl.BlockSpec((L, nv), lambda c: (c, v_blk + 1)),
            pl.BlockSpec((L, LANES), lambda c: (c, 0)),
            pl.BlockSpec((gr, L), lambda c: (0, c)),
            pl.BlockSpec((CONV_WIDTH, 2 * nqk), full),
            pl.BlockSpec((1, 2 * nqk), full),
            pl.BlockSpec((1, LANES), full),
            pl.BlockSpec((gr, 1), full),
            pl.BlockSpec((1, nv), full),
        ],
        out_specs=pl.BlockSpec((L, nv), lambda c: (c, 0)),
        out_shape=jax.ShapeDtypeStruct((s, nv), BF16),
        scratch_shapes=[
            pltpu.VMEM((L + SUBLANES, 2 * nqk), F32),
            pltpu.VMEM((ML_HEADS, ML_QK_DIM, ML_V_DIM + LANES), F32),
            pltpu.VMEM((ML_HEADS, SUBLANES, LANES), F32),
        ],
        compiler_params=_params(("arbitrary",)),
        name="mlstm_scan",
    )(z_main, z_main, z_main, z_main, gcol, grow, conv_w, conv_b, gb_row, gb_col, norm_g_ml)


def _outproj_kernel(x_ref, ysb_ref, yml_ref, wsb_ref, wml_ref, gate_ref, g_ref, b_ref, o_ref,
                    *, alpha):
    m = (jnp.dot(ysb_ref[...], wsb_ref[...], preferred_element_type=F32)
         + jnp.dot(yml_ref[...], wml_ref[...], preferred_element_type=F32))
    y = alpha * x_ref[...] + (1.0 + gate_ref[...]) * m
    o_ref[...] = _layer_norm_rows(y, g_ref[...], b_ref[...])


def _outproj(x, y_sb, y_ml, w_out, layer, gate, g, b, alpha):
    s, d = x.shape
    sbw = y_sb.shape[1]
    assert y_ml.shape[1] == sbw and w_out.shape[1] == 2 * sbw
    row = lambda i: (i, 0)
    full = lambda i: (0, 0)
    return pl.pallas_call(
        functools.partial(_outproj_kernel, alpha=alpha),
        grid=(s // OUT_TM,),
        in_specs=[
            pl.BlockSpec((OUT_TM, d), row),
            pl.BlockSpec((OUT_TM, y_sb.shape[1]), row),
            pl.BlockSpec((OUT_TM, y_ml.shape[1]), row),
            pl.BlockSpec((None, sbw, d), lambda i: (layer, 0, 0)),
            pl.BlockSpec((None, sbw, d), lambda i: (layer, 1, 0)),
            pl.BlockSpec((1, d), full),
            pl.BlockSpec((1, d), full),
            pl.BlockSpec((1, d), full),
        ],
        out_specs=pl.BlockSpec((OUT_TM, d), row),
        out_shape=jax.ShapeDtypeStruct((s, d), F32),
        compiler_params=_params(("parallel",)),
        name="mixer_out_proj",
    )(x, y_sb, y_ml, w_out, w_out, gate, g, b)


def kernel(x, c, ada_w, ada_b, ln_g, ln_b, ffn1_w1, ffn1_w2, mix_w_in, mlstm_conv_w,
           mlstm_conv_b, mlstm_gate_b, mix_norm_g, mix_w_out, ffn2_w1, ffn2_w2):
    bsz, s, d = x.shape
    depth = ada_w.shape[0]
    assert bsz == 1, "kernels are written for a single sequence"
    assert s % max(FFN_TM, PROJ_TM, SB_BLK, ML_CHUNK, OUT_TM) == 0
    alpha = (2 * depth) ** 0.25
    sbw = SB_HEADS * SB_HEAD_DIM
    n_main = mix_w_in.shape[2] - 2 * ML_HEADS
    gate_rows = 2 * SUBLANES

    xs = x.reshape(s, d)
    mod = _adaln(c.reshape(d, 1), ada_w, ada_b).reshape(depth, N_SUB, N_MOD, 1, d)
    col_scale = jnp.concatenate(
        [jnp.full((1, sbw), SB_HEAD_DIM ** -0.5 * LOG2E, F32), jnp.ones((1, n_main - sbw), F32)],
        axis=1)

    f1w1, f1w2 = ffn1_w1.astype(BF16), ffn1_w2.astype(BF16)
    f2w1, f2w2 = ffn2_w1.astype(BF16), ffn2_w2.astype(BF16)
    w_in, w_out = mix_w_in.astype(BF16), mix_w_out.astype(BF16)

    for l in range(depth):
        xs = _ffn(xs, mod[l, 0, 0], mod[l, 0, 1], mod[l, 0, 2], f1w1, f1w2, l,
                  ln_g[l, 0][None], ln_b[l, 0][None], alpha)

        w_gate = w_in[l, :, n_main:]
        w_gate_c = jnp.pad(w_gate, ((0, 0), (0, LANES - 2 * ML_HEADS)))
        w_gate_t = jnp.pad(w_gate.T, ((0, gate_rows - 2 * ML_HEADS), (0, 0)))
        z_main, gcol, grow = _proj(xs, mod[l, 1, 0], mod[l, 1, 1], w_in, l, col_scale,
                                   w_gate_c, w_gate_t)
        gb = mlstm_gate_b[l]
        gb_row = jnp.pad(gb, (0, LANES - 2 * ML_HEADS))[None]
        gb_col = jnp.pad(gb, (0, gate_rows - 2 * ML_HEADS))[:, None]
        y_sb = _sb_attention(z_main, mix_norm_g[l][None, :sbw])
        y_ml = _mlstm(z_main, gcol, grow, mlstm_conv_w[l], mlstm_conv_b[l][None],
                      gb_row, gb_col, mix_norm_g[l][None, sbw:])
        xs = _outproj(xs, y_sb, y_ml, w_out, l, mod[l, 1, 2],
                      ln_g[l, 1][None], ln_b[l, 1][None], alpha)

        xs = _ffn(xs, mod[l, 2, 0], mod[l, 2, 1], mod[l, 2, 2], f2w1, f2w2, l,
                  ln_g[l, 2][None], ln_b[l, 2][None], alpha)

    return xs.reshape(bsz, s, d)
```

```python
import functools

import jax
import jax.numpy as jnp
from jax import lax
from jax.experimental import pallas as pl
from jax.experimental.pallas import tpu as pltpu

F32 = jnp.float32
BF16 = jnp.bfloat16

SB_HEADS = 8
SB_HEAD_DIM = 128
ML_HEADS = 4
ML_QK_DIM = 128
ML_V_DIM = 256
CONV_WIDTH = 4
N_SUB = 3
N_MOD = 3
LN_EPS = 1e-5
LANES = 128
SUBLANES = 8
NEG_BIG = -1e30
LOG2E = 1.4426950408889634
F32_EXP2_UNDERFLOW = 150.0

ADA_TN = 1024
FFN_TM = 1024
FFN_TF = 512
PROJ_TM = 1024
PROJ_TN = 1024
SB_BLK = 256
SB_PACK = 2
ML_CHUNK = 256
OUT_TM = 512
VMEM_LIMIT = 56 * 1024 * 1024
FFN_VMEM_LIMIT = 60 * 1024 * 1024


def _params(sem, vmem_limit=VMEM_LIMIT):
    return pltpu.CompilerParams(dimension_semantics=sem, vmem_limit_bytes=vmem_limit)


def _layer_norm_rows(y, g, b):
    mu = jnp.mean(y, axis=-1, keepdims=True)
    d = y - mu
    var = jnp.mean(d * d, axis=-1, keepdims=True)
    return d * lax.rsqrt(var + LN_EPS) * g + b


def _split_bf16(v):
    hi = v.astype(BF16)
    lo = (v - hi.astype(F32)).astype(BF16)
    return hi, lo


def _log_sigmoid(z):
    return jnp.minimum(z, 0.0) - jnp.log(1.0 + jnp.exp(-jnp.abs(z)))


def _adaln_kernel(c_ref, w_ref, b_ref, o_ref):
    cv = c_ref[...]
    sc = cv * jax.nn.sigmoid(cv)
    o_ref[0] = jnp.sum(w_ref[0] * sc, axis=0, keepdims=True) + b_ref[0]


def _adaln(c_col, ada_w, ada_b):
    depth, d, n = ada_w.shape
    return pl.pallas_call(
        _adaln_kernel,
        grid=(depth, n // ADA_TN),
        in_specs=[
            pl.BlockSpec((d, 1), lambda l, j: (0, 0)),
            pl.BlockSpec((1, d, ADA_TN), lambda l, j: (l, 0, j)),
            pl.BlockSpec((1, 1, ADA_TN), lambda l, j: (l, 0, j)),
        ],
        out_specs=pl.BlockSpec((1, 1, ADA_TN), lambda l, j: (l, 0, j)),
        out_shape=jax.ShapeDtypeStruct((depth, 1, n), F32),
        compiler_params=_params(("arbitrary", "arbitrary")),
        name="adaln_gemv",
    )(c_col, ada_w, ada_b.reshape(depth, 1, n))


def _ffn_kernel(x_ref, shift_ref, scale_ref, gate_ref, w1a_ref, w1u_ref, w2_ref,
                g_ref, b_ref, o_ref, *, alpha, n_f):
    j = pl.program_id(1)

    @pl.when(j == 0)
    def _():
        o_ref[...] = jnp.zeros_like(o_ref)

    h = (x_ref[...] * (1.0 + scale_ref[...]) + shift_ref[...]).astype(BF16)
    a = jnp.dot(h, w1a_ref[...], preferred_element_type=F32)
    u = jnp.dot(h, w1u_ref[...], preferred_element_type=F32)
    act = (a * jax.nn.sigmoid(a) * u).astype(BF16)
    o_ref[...] += jnp.dot(act, w2_ref[...], preferred_element_type=F32)

    @pl.when(j == n_f - 1)
    def _():
        y = alpha * x_ref[...] + (0.5 * (1.0 + gate_ref[...])) * o_ref[...]
        o_ref[...] = _layer_norm_rows(y, g_ref[...], b_ref[...])


def _ffn(x, shift, scale, gate, w1, w2, layer, g, b, alpha):
    s, d = x.shape
    d_ff = w2.shape[1]
    n_f = d_ff // FFN_TF
    row = lambda i, j: (i, 0)
    vec = lambda i, j: (0, 0)
    return pl.pallas_call(
        functools.partial(_ffn_kernel, alpha=alpha, n_f=n_f),
        grid=(s // FFN_TM, n_f),
        in_specs=[
            pl.BlockSpec((FFN_TM, d), row),
            pl.BlockSpec((1, d), vec),
            pl.BlockSpec((1, d), vec),
            pl.BlockSpec((1, d), vec),
            pl.BlockSpec((None, d, FFN_TF), lambda i, j: (layer, 0, j)),
            pl.BlockSpec((None, d, FFN_TF), lambda i, j: (layer, 0, j + n_f)),
            pl.BlockSpec((None, FFN_TF, d), lambda i, j: (layer, j, 0)),
            pl.BlockSpec((1, d), vec),
            pl.BlockSpec((1, d), vec),
        ],
        out_specs=pl.BlockSpec((FFN_TM, d), row),
        out_shape=jax.ShapeDtypeStruct((s, d), F32),
        compiler_params=_params(("parallel", "arbitrary"), FFN_VMEM_LIMIT),
        name="ffn_sublayer",
    )(x, shift, scale, gate, w1, w1, w2, g, b)


def _proj_kernel(x_ref, shift_ref, scale_ref, w_ref, cs_ref, wg_ref, wgt_ref,
                 z_ref, gcol_ref, grow_ref, h_ref):
    j = pl.program_id(1)

    @pl.when(j == 0)
    def _():
        h = (x_ref[...] * (1.0 + scale_ref[...]) + shift_ref[...]).astype(BF16)
        h_ref[...] = h
        gcol_ref[...] = jnp.dot(h, wg_ref[...], preferred_element_type=F32)
        grow_ref[...] = lax.dot_general(wgt_ref[...], h, (((1,), (1,)), ((), ())),
                                        preferred_element_type=F32)

    z = jnp.dot(h_ref[...], w_ref[...], preferred_element_type=F32)
    z_ref[...] = (z * cs_ref[...]).astype(BF16)


def _proj(x, shift, scale, w_in, layer, col_scale, w_gate, w_gate_t):
    s, d = x.shape
    n = col_scale.shape[1]
    gr = w_gate_t.shape[0]
    row = lambda i, j: (i, 0)
    vec = lambda i, j: (0, 0)
    return pl.pallas_call(
        _proj_kernel,
        grid=(s // PROJ_TM, n // PROJ_TN),
        in_specs=[
            pl.BlockSpec((PROJ_TM, d), row),
            pl.BlockSpec((1, d), vec),
            pl.BlockSpec((1, d), vec),
            pl.BlockSpec((None, d, PROJ_TN), lambda i, j: (layer, 0, j)),
            pl.BlockSpec((1, PROJ_TN), lambda i, j: (0, j)),
            pl.BlockSpec((d, LANES), vec),
            pl.BlockSpec((gr, d), vec),
        ],
        out_specs=[
            pl.BlockSpec((PROJ_TM, PROJ_TN), lambda i, j: (i, j)),
            pl.BlockSpec((PROJ_TM, LANES), row),
            pl.BlockSpec((gr, PROJ_TM), lambda i, j: (0, i)),
        ],
        out_shape=[
            jax.ShapeDtypeStruct((s, n), BF16),
            jax.ShapeDtypeStruct((s, LANES), F32),
            jax.ShapeDtypeStruct((gr, s), F32),
        ],
        scratch_shapes=[pltpu.VMEM((PROJ_TM, d), BF16)],
        compiler_params=_params(("parallel", "arbitrary")),
        name="mixer_in_proj",
    )(x, shift, scale, w_in, col_scale, w_gate, w_gate_t)


def _sb_kernel(q_ref, k_ref, v_ref, g_ref, o_ref):
    blk = SB_BLK
    hd = SB_HEAD_DIM
    i = pl.program_id(1)
    lanes = [slice(p * hd, (p + 1) * hd) for p in range(SB_PACK)]
    qs = [q_ref[:, ln] for ln in lanes]
    r_id = lax.broadcasted_iota(jnp.int32, (blk, blk), 0)
    c_id = lax.broadcasted_iota(jnp.int32, (blk, blk), 1)
    later = (r_id > c_id).astype(BF16)
    causal = c_id < r_id

    def visit(j, used, acc, diagonal):
        start = pl.multiple_of(j * blk, blk)
        new_used, new_acc = [], []
        for p, ln in enumerate(lanes):
            ks = k_ref[pl.ds(start, blk), ln]
            vs = v_ref[pl.ds(start, blk), ln]
            z = lax.dot_general(qs[p], ks, (((1,), (1,)), ((), ())), preferred_element_type=F32)
            if diagonal:
                z = jnp.where(causal, z, NEG_BIG)
            sp = jnp.maximum(z, 0.0) + jnp.log(1.0 + jnp.exp2(-jnp.abs(z))) * LOG2E
            after = jnp.dot(sp.astype(BF16), later,
                            preferred_element_type=F32)
            a = jnp.exp2(z - sp - after - used[p])
            new_acc.append(acc[p] + jnp.dot(a.astype(BF16), vs, preferred_element_type=F32))
            new_used.append(used[p] + (after[:, 0:1] + sp[:, 0:1]))
        return tuple(new_used), tuple(new_acc)

    def first_pair():
        r2 = lax.broadcasted_iota(jnp.int32, (blk, 2 * blk), 0)
        c2 = lax.broadcasted_iota(jnp.int32, (blk, 2 * blk), 1)
        allowed = c2 < r2 + blk
        start = pl.multiple_of((i - 1) * blk, blk)
        new_used, new_acc = [], []
        for p, ln in enumerate(lanes):
            kw = k_ref[pl.ds(start, 2 * blk), ln]
            vw = v_ref[pl.ds(start, 2 * blk), ln]
            z = lax.dot_general(qs[p], kw, (((1,), (1,)), ((), ())), preferred_element_type=F32)
            z = jnp.where(allowed, z, NEG_BIG)
            sp = jnp.maximum(z, 0.0) + jnp.log(1.0 + jnp.exp2(-jnp.abs(z))) * LOG2E
            sp_b = sp.astype(BF16)
            after_hi = jnp.dot(sp_b[:, blk:], later, preferred_element_type=F32)
            total_hi = after_hi[:, 0:1] + sp[:, blk:blk + 1]
            after_lo = jnp.dot(sp_b[:, :blk], later, preferred_element_type=F32) + total_hi
            after = jnp.concatenate([after_lo, after_hi], axis=1)
            a = jnp.exp2(z - sp - after)
            new_acc.append(jnp.dot(a.astype(BF16), vw, preferred_element_type=F32))
            new_used.append(after_lo[:, 0:1] + sp[:, 0:1])
        return tuple(new_used), tuple(new_acc)

    def finish(acc):
        for p, ln in enumerate(lanes):
            mu = jnp.mean(acc[p], axis=-1, keepdims=True)
            d = acc[p] - mu
            var = jnp.mean(d * d, axis=-1, keepdims=True)
            o_ref[:, ln] = (d * lax.rsqrt(var + LN_EPS) * g_ref[:, ln]).astype(o_ref.dtype)

    @pl.when(i == 0)
    def _():
        used0 = tuple(jnp.zeros((blk, 1), F32) for _ in lanes)
        acc0 = tuple(jnp.zeros((blk, hd), F32) for _ in lanes)
        _, acc = visit(0, used0, acc0, True)
        finish(acc)

    @pl.when(i > 0)
    def _():
        used, acc = first_pair()

        def more(state):
            n, used, _ = state
            least = functools.reduce(jnp.minimum, used)
            return jnp.logical_and(n < i - 1, jnp.min(least) < F32_EXP2_UNDERFLOW)

        def body(state):
            n, used, acc = state
            used, acc = visit(i - 2 - n, used, acc, False)
            return n + 1, used, acc

        _, _, acc = lax.while_loop(more, body, (jnp.int32(0), used, acc))
        finish(acc)


def _sb_attention(z_main, norm_g_sb):
    s = z_main.shape[0]
    w = SB_PACK * SB_HEAD_DIM
    groups = SB_HEADS // SB_PACK
    return pl.pallas_call(
        _sb_kernel,
        grid=(groups, s // SB_BLK),
        in_specs=[
            pl.BlockSpec((SB_BLK, w), lambda h, i: (i, h)),
            pl.BlockSpec((s, w), lambda h, i: (0, groups + h)),
            pl.BlockSpec((s, w), lambda h, i: (0, 2 * groups + h)),
            pl.BlockSpec((1, w), lambda h, i: (0, h)),
        ],
        out_specs=pl.BlockSpec((SB_BLK, w), lambda h, i: (i, h)),
        out_shape=jax.ShapeDtypeStruct((s, SB_HEADS * SB_HEAD_DIM), BF16),
        compiler_params=_params(("parallel", "arbitrary")),
        name="stickbreak_attn",
    )(z_main, z_main, z_main, norm_g_sb)


def _mlstm_kernel(zq_ref, zk_ref, zv_ref, zo_ref, gcol_ref, grow_ref, cw_ref, cb_ref,
                  gb_row_ref, gb_col_ref, ng_ref, y_ref, ubuf_ref, c_ref, m_ref):
    L = ML_CHUNK
    nqk = ML_HEADS * ML_QK_DIM
    step = pl.program_id(0)

    @pl.when(step == 0)
    def _():
        ubuf_ref[0:SUBLANES, :] = jnp.zeros((SUBLANES, 2 * nqk), F32)
        c_ref[...] = jnp.zeros_like(c_ref)
        m_ref[...] = jnp.zeros_like(m_ref)

    ubuf_ref[SUBLANES:SUBLANES + L, 0:nqk] = zq_ref[...].astype(F32)
    ubuf_ref[SUBLANES:SUBLANES + L, nqk:2 * nqk] = zk_ref[...].astype(F32)
    conv = cb_ref[...]
    for tap in range(CONV_WIDTH):
        off = SUBLANES - (CONV_WIDTH - 1) + tap
        conv = conv + ubuf_ref[off:off + L, :] * cw_ref[tap:tap + 1, :]
    ubuf_ref[0:SUBLANES, :] = ubuf_ref[L:L + SUBLANES, :]
    qk = conv * jax.nn.sigmoid(conv)

    gc = gcol_ref[...] + gb_row_ref[...]
    gr = grow_ref[...] + gb_col_ref[...]
    r_id = lax.broadcasted_iota(jnp.int32, (L, L), 0)
    c_id = lax.broadcasted_iota(jnp.int32, (L, L), 1)
    causal = c_id <= r_id
    tri = causal.astype(BF16)
    tri_t = (r_id <= c_id).astype(BF16)
    lfc_hi, lfc_lo = _split_bf16(_log_sigmoid(gc))
    lfr_hi, lfr_lo = _split_bf16(_log_sigmoid(gr))
    bc = (jnp.dot(tri, lfc_hi, preferred_element_type=F32)
          + jnp.dot(tri, lfc_lo, preferred_element_type=F32))
    br = (jnp.dot(lfr_hi, tri_t, preferred_element_type=F32)
          + jnp.dot(lfr_lo, tri_t, preferred_element_type=F32))

    lane = lax.broadcasted_iota(jnp.int32, (L, LANES), 1)
    ones_col = (lane == 0).astype(BF16)

    for h in range(ML_HEADS):
        b_col = bc[:, ML_HEADS + h:ML_HEADS + h + 1]
        i_col = gc[:, h:h + 1]
        b_row = br[ML_HEADS + h:ML_HEADS + h + 1, :]
        i_row = gr[h:h + 1, :]
        m_prev = m_ref[h][0:1, 0:1]

        log_d = jnp.where(causal, b_col + (i_row - b_row), NEG_BIG)
        log_g = b_col + m_prev
        m_t = jnp.maximum(jnp.max(log_d, axis=-1, keepdims=True), log_g)
        p = jnp.exp(log_d - m_t)
        g = jnp.exp(log_g - m_t)

        qh = (qk[:, h * ML_QK_DIM:(h + 1) * ML_QK_DIM] * (ML_QK_DIM ** -0.5)).astype(BF16)
        kf = qk[:, nqk + h * ML_QK_DIM:nqk + (h + 1) * ML_QK_DIM]
        kh = kf.astype(BF16)
        vext = jnp.concatenate([zv_ref[:, h * ML_V_DIM:(h + 1) * ML_V_DIM], ones_col], axis=-1)

        sc = lax.dot_general(qh, kh, (((1,), (1,)), ((), ())), preferred_element_type=F32) * p
        state = c_ref[h]
        num = (jnp.dot(sc.astype(BF16), vext, preferred_element_type=F32)
               + g * jnp.dot(qh, state.astype(BF16), preferred_element_type=F32))
        den = num[:, ML_V_DIM:ML_V_DIM + 1]
        hv = num[:, 0:ML_V_DIM] * (1.0 / jnp.maximum(jnp.abs(den), jnp.exp(-m_t)))

        b_last = b_col[L - 1:L, :]
        log_w = b_last - b_col + i_col
        m_new = jnp.maximum(b_last + m_prev, jnp.max(log_w, axis=0, keepdims=True))
        decay = jnp.exp(b_last + m_prev - m_new)
        kw = (kf * jnp.exp(log_w - m_new)).astype(BF16)
        c_ref[h] = decay * state + lax.dot_general(
            kw, vext, (((0,), (0,)), ((), ())), preferred_element_type=F32)
        m_ref[h] = jnp.broadcast_to(m_new, (SUBLANES, LANES))

        mu = jnp.mean(hv, axis=-1, keepdims=True)
        d = hv - mu
        var = jnp.mean(d * d, axis=-1, keepdims=True)
        hn = d * lax.rsqrt(var + LN_EPS) * ng_ref[:, h * ML_V_DIM:(h + 1) * ML_V_DIM]
        og = jax.nn.sigmoid(zo_ref[:, h * ML_V_DIM:(h + 1) * ML_V_DIM].astype(F32))
        y_ref[:, h * ML_V_DIM:(h + 1) * ML_V_DIM] = (og * hn).astype(y_ref.dtype)


def _mlstm(z_main, gcol, grow, conv_w, conv_b, gb_row, gb_col, norm_g_ml):
    s = z_main.shape[0]
    L = ML_CHUNK
    nqk = ML_HEADS * ML_QK_DIM
    nv = ML_HEADS * ML_V_DIM
    sbw = SB_HEADS * SB_HEAD_DIM
    q_blk = (3 * sbw) // nqk
    v_blk = (3 * sbw + 2 * nqk) // nv
    gr = grow.shape[0]
    full = lambda c: (0, 0)
    return pl.pallas_call(
        _mlstm_kernel,
        grid=(s // L,),
        in_specs=[
            pl.BlockSpec((L, nqk), lambda c: (c, q_blk)),
            pl.BlockSpec((L, nqk), lambda c: (c, q_blk + 1)),
            pl.BlockSpec((L, nv), lambda c: (c, v_blk)),
            pl.BlockSpec((L, nv), lambda c: (c, v_blk + 1)),
            pl.BlockSpec((L, LANES), lambda c: (c, 0)),
            pl.BlockSpec((gr, L), lambda c: (0, c)),
            pl.BlockSpec((CONV_WIDTH, 2 * nqk), full),
            pl.BlockSpec((1, 2 * nqk), full),
            pl.BlockSpec((1, LANES), full),
            pl.BlockSpec((gr, 1), full),
            pl.BlockSpec((1, nv), full),
        ],
        out_specs=pl.BlockSpec((L, nv), lambda c: (c, 0)),
        out_shape=jax.ShapeDtypeStruct((s, nv), BF16),
        scratch_shapes=[
            pltpu.VMEM((L + SUBLANES, 2 * nqk), F32),
            pltpu.VMEM((ML_HEADS, ML_QK_DIM, ML_V_DIM + LANES), F32),
            pltpu.VMEM((ML_HEADS, SUBLANES, LANES), F32),
        ],
        compiler_params=_params(("arbitrary",)),
        name="mlstm_scan",
    )(z_main, z_main, z_main, z_main, gcol, grow, conv_w, conv_b, gb_row, gb_col, norm_g_ml)


def _outproj_kernel(x_ref, ysb_ref, yml_ref, wsb_ref, wml_ref, gate_ref, g_ref, b_ref, o_ref,
                    *, alpha):
    m = (jnp.dot(ysb_ref[...], wsb_ref[...], preferred_element_type=F32)
         + jnp.dot(yml_ref[...], wml_ref[...], preferred_element_type=F32))
    y = alpha * x_ref[...] + (1.0 + gate_ref[...]) * m
    o_ref[...] = _layer_norm_rows(y, g_ref[...], b_ref[...])


def _outproj(x, y_sb, y_ml, w_out, layer, gate, g, b, alpha):
    s, d = x.shape
    sbw = y_sb.shape[1]
    assert y_ml.shape[1] == sbw and w_out.shape[1] == 2 * sbw
    row = lambda i: (i, 0)
    full = lambda i: (0, 0)
    return pl.pallas_call(
        functools.partial(_outproj_kernel, alpha=alpha),
        grid=(s // OUT_TM,),
        in_specs=[
            pl.BlockSpec((OUT_TM, d), row),
            pl.BlockSpec((OUT_TM, y_sb.shape[1]), row),
            pl.BlockSpec((OUT_TM, y_ml.shape[1]), row),
            pl.BlockSpec((None, sbw, d), lambda i: (layer, 0, 0)),
            pl.BlockSpec((None, sbw, d), lambda i: (layer, 1, 0)),
            pl.BlockSpec((1, d), full),
            pl.BlockSpec((1, d), full),
            pl.BlockSpec((1, d), full),
        ],
        out_specs=pl.BlockSpec((OUT_TM, d), row),
        out_shape=jax.ShapeDtypeStruct((s, d), F32),
        compiler_params=_params(("parallel",)),
        name="mixer_out_proj",
    )(x, y_sb, y_ml, w_out, w_out, gate, g, b)


def kernel(x, c, ada_w, ada_b, ln_g, ln_b, ffn1_w1, ffn1_w2, mix_w_in, mlstm_conv_w,
           mlstm_conv_b, mlstm_gate_b, mix_norm_g, mix_w_out, ffn2_w1, ffn2_w2):
    bsz, s, d = x.shape
    depth = ada_w.shape[0]
    assert bsz == 1, "kernels are written for a single sequence"
    assert s % max(FFN_TM, PROJ_TM, SB_BLK, ML_CHUNK, OUT_TM) == 0
    alpha = (2 * depth) ** 0.25
    sbw = SB_HEADS * SB_HEAD_DIM
    n_main = mix_w_in.shape[2] - 2 * ML_HEADS
    gate_rows = 2 * SUBLANES

    xs = x.reshape(s, d)
    mod = _adaln(c.reshape(d, 1), ada_w, ada_b).reshape(depth, N_SUB, N_MOD, 1, d)
    col_scale = jnp.concatenate(
        [jnp.full((1, sbw), SB_HEAD_DIM ** -0.5 * LOG2E, F32), jnp.ones((1, n_main - sbw), F32)],
        axis=1)

    f1w1, f1w2 = ffn1_w1.astype(BF16), ffn1_w2.astype(BF16)
    f2w1, f2w2 = ffn2_w1.astype(BF16), ffn2_w2.astype(BF16)
    w_in = mix_w_in[:, :, :n_main].astype(BF16)
    w_gates = mix_w_in[:, :, n_main:].astype(BF16)
    w_out = mix_w_out.astype(BF16)

    for l in range(depth):
        xs = _ffn(xs, mod[l, 0, 0], mod[l, 0, 1], mod[l, 0, 2], f1w1, f1w2, l,
                  ln_g[l, 0][None], ln_b[l, 0][None], alpha)

        w_gate = w_gates[l]
        w_gate_c = jnp.pad(w_gate, ((0, 0), (0, LANES - 2 * ML_HEADS)))
        w_gate_t = jnp.pad(w_gate.T, ((0, gate_rows - 2 * ML_HEADS), (0, 0)))
        z_main, gcol, grow = _proj(xs, mod[l, 1, 0], mod[l, 1, 1], w_in, l, col_scale,
                                   w_gate_c, w_gate_t)
        gb = mlstm_gate_b[l]
        gb_row = jnp.pad(gb, (0, LANES - 2 * ML_HEADS))[None]
        gb_col = jnp.pad(gb, (0, gate_rows - 2 * ML_HEADS))[:, None]
        y_sb = _sb_attention(z_main, mix_norm_g[l][None, :sbw])
        y_ml = _mlstm(z_main, gcol, grow, mlstm_conv_w[l], mlstm_conv_b[l][None],
                      gb_row, gb_col, mix_norm_g[l][None, sbw:])
        xs = _outproj(xs, y_sb, y_ml, w_out, l, mod[l, 1, 2],
                      ln_g[l, 1][None], ln_b[l, 1][None], alpha)

        xs = _ffn(xs, mod[l, 2, 0], mod[l, 2, 1], mod[l, 2, 2], f2w1, f2w2, l,
                  ln_g[l, 2][None], ln_b[l, 2][None], alpha)

    return xs.reshape(bsz, s, d)
```

```python
import functools

import jax
import jax.numpy as jnp
from jax import lax
from jax.experimental import pallas as pl
from jax.experimental.pallas import tpu as pltpu

F32 = jnp.float32
BF16 = jnp.bfloat16

SB_HEADS = 8
SB_HEAD_DIM = 128
ML_HEADS = 4
ML_QK_DIM = 128
ML_V_DIM = 256
CONV_WIDTH = 4
N_SUB = 3
N_MOD = 3
LN_EPS = 1e-5
LANES = 128
SUBLANES = 8
NEG_BIG = -1e30
LOG2E = 1.4426950408889634
F32_EXP2_UNDERFLOW = 150.0

ADA_TN = 1024
FFN_TM = 1024
FFN_TF = 512
PROJ_TM = 1024
PROJ_TN = 1024
SB_BLK = 256
SB_PACK = 2
ML_CHUNK = 256
OUT_TM = 512
VMEM_LIMIT = 56 * 1024 * 1024
FFN_VMEM_LIMIT = 60 * 1024 * 1024


def _params(sem, vmem_limit=VMEM_LIMIT):
    return pltpu.CompilerParams(dimension_semantics=sem, vmem_limit_bytes=vmem_limit)


def _layer_norm_rows(y, g, b):
    mu = jnp.mean(y, axis=-1, keepdims=True)
    d = y - mu
    var = jnp.mean(d * d, axis=-1, keepdims=True)
    return d * lax.rsqrt(var + LN_EPS) * g + b


def _split_bf16(v):
    hi = v.astype(BF16)
    lo = (v - hi.astype(F32)).astype(BF16)
    return hi, lo


def _log_sigmoid(z):
    return jnp.minimum(z, 0.0) - jnp.log(1.0 + jnp.exp(-jnp.abs(z)))


def _adaln_kernel(c_ref, w_ref, b_ref, o_ref):
    cv = c_ref[...]
    sc = cv * jax.nn.sigmoid(cv)
    o_ref[0] = jnp.sum(w_ref[0] * sc, axis=0, keepdims=True) + b_ref[0]


def _adaln(c_col, ada_w, ada_b):
    depth, d, n = ada_w.shape
    return pl.pallas_call(
        _adaln_kernel,
        grid=(depth, n // ADA_TN),
        in_specs=[
            pl.BlockSpec((d, 1), lambda l, j: (0, 0)),
            pl.BlockSpec((1, d, ADA_TN), lambda l, j: (l, 0, j)),
            pl.BlockSpec((1, 1, ADA_TN), lambda l, j: (l, 0, j)),
        ],
        out_specs=pl.BlockSpec((1, 1, ADA_TN), lambda l, j: (l, 0, j)),
        out_shape=jax.ShapeDtypeStruct((depth, 1, n), F32),
        compiler_params=_params(("arbitrary", "arbitrary")),
        name="adaln_gemv",
    )(c_col, ada_w, ada_b.reshape(depth, 1, n))


def _ffn_kernel(x_ref, shift_ref, scale_ref, gate_ref, w1a_ref, w1u_ref, w2_ref,
                g_ref, b_ref, o_ref, *, alpha, n_f):
    j = pl.program_id(1)

    @pl.when(j == 0)
    def _():
        o_ref[...] = jnp.zeros_like(o_ref)

    h = (x_ref[...] * (1.0 + scale_ref[...]) + shift_ref[...]).astype(BF16)
    a = jnp.dot(h, w1a_ref[...], preferred_element_type=F32)
    u = jnp.dot(h, w1u_ref[...], preferred_element_type=F32)
    act = (a * jax.nn.sigmoid(a) * u).astype(BF16)
    o_ref[...] += jnp.dot(act, w2_ref[...], preferred_element_type=F32)

    @pl.when(j == n_f - 1)
    def _():
        y = alpha * x_ref[...] + (0.5 * (1.0 + gate_ref[...])) * o_ref[...]
        o_ref[...] = _layer_norm_rows(y, g_ref[...], b_ref[...])


def _ffn(x, shift, scale, gate, w1, w2, layer, g, b, alpha):
    s, d = x.shape
    d_ff = w2.shape[1]
    n_f = d_ff // FFN_TF
    row = lambda i, j: (i, 0)
    vec = lambda i, j: (0, 0)
    return pl.pallas_call(
        functools.partial(_ffn_kernel, alpha=alpha, n_f=n_f),
        grid=(s // FFN_TM, n_f),
        in_specs=[
            pl.BlockSpec((FFN_TM, d), row),
            pl.BlockSpec((1, d), vec),
            pl.BlockSpec((1, d), vec),
            pl.BlockSpec((1, d), vec),
            pl.BlockSpec((None, d, FFN_TF), lambda i, j: (layer, 0, j)),
            pl.BlockSpec((None, d, FFN_TF), lambda i, j: (layer, 0, j + n_f)),
            pl.BlockSpec((None, FFN_TF, d), lambda i, j: (layer, j, 0)),
            pl.BlockSpec((1, d), vec),
            pl.BlockSpec((1, d), vec),
        ],
        out_specs=pl.BlockSpec((FFN_TM, d), row),
        out_shape=jax.ShapeDtypeStruct((s, d), F32),
        compiler_params=_params(("parallel", "arbitrary"), FFN_VMEM_LIMIT),
        name="ffn_sublayer",
    )(x, shift, scale, gate, w1, w1, w2, g, b)


def _proj_kernel(x_ref, shift_ref, scale_ref, w_ref, cs_ref, wg_ref, wgt_ref,
                 z_ref, gcol_ref, grow_ref, h_ref):
    j = pl.program_id(1)

    @pl.when(j == 0)
    def _():
        h = (x_ref[...] * (1.0 + scale_ref[...]) + shift_ref[...]).astype(BF16)
        h_ref[...] = h
        gcol_ref[...] = jnp.dot(h, wg_ref[...], preferred_element_type=F32)
        grow_ref[...] = lax.dot_general(wgt_ref[...], h, (((1,), (1,)), ((), ())),
                                        preferred_element_type=F32)

    z = jnp.dot(h_ref[...], w_ref[...].astype(BF16), preferred_element_type=F32)
    z_ref[...] = (z * cs_ref[...]).astype(BF16)


def _proj(x, shift, scale, w_in, layer, col_scale, w_gate, w_gate_t):
    s, d = x.shape
    n = col_scale.shape[1]
    gr = w_gate_t.shape[0]
    row = lambda i, j: (i, 0)
    vec = lambda i, j: (0, 0)
    return pl.pallas_call(
        _proj_kernel,
        grid=(s // PROJ_TM, n // PROJ_TN),
        in_specs=[
            pl.BlockSpec((PROJ_TM, d), row),
            pl.BlockSpec((1, d), vec),
            pl.BlockSpec((1, d), vec),
            pl.BlockSpec((None, d, PROJ_TN), lambda i, j: (layer, 0, j)),
            pl.BlockSpec((1, PROJ_TN), lambda i, j: (0, j)),
            pl.BlockSpec((d, LANES), vec),
            pl.BlockSpec((gr, d), vec),
        ],
        out_specs=[
            pl.BlockSpec((PROJ_TM, PROJ_TN), lambda i, j: (i, j)),
            pl.BlockSpec((PROJ_TM, LANES), row),
            pl.BlockSpec((gr, PROJ_TM), lambda i, j: (0, i)),
        ],
        out_shape=[
            jax.ShapeDtypeStruct((s, n), BF16),
            jax.ShapeDtypeStruct((s, LANES), F32),
            jax.ShapeDtypeStruct((gr, s), F32),
        ],
        scratch_shapes=[pltpu.VMEM((PROJ_TM, d), BF16)],
        compiler_params=_params(("parallel", "arbitrary")),
        name="mixer_in_proj",
    )(x, shift, scale, w_in, col_scale, w_gate, w_gate_t)


def _sb_kernel(q_ref, k_ref, v_ref, g_ref, o_ref):
    blk = SB_BLK
    hd = SB_HEAD_DIM
    i = pl.program_id(1)
    lanes = [slice(p * hd, (p + 1) * hd) for p in range(SB_PACK)]
    qs = [q_ref[:, ln] for ln in lanes]
    r_id = lax.broadcasted_iota(jnp.int32, (blk, blk), 0)
    c_id = lax.broadcasted_iota(jnp.int32, (blk, blk), 1)
    later = (r_id > c_id).astype(BF16)
    causal = c_id < r_id

    def visit(j, used, acc, diagonal):
        start = pl.multiple_of(j * blk, blk)
        new_used, new_acc = [], []
        for p, ln in enumerate(lanes):
            ks = k_ref[pl.ds(start, blk), ln]
            vs = v_ref[pl.ds(start, blk), ln]
            z = lax.dot_general(qs[p], ks, (((1,), (1,)), ((), ())), preferred_element_type=F32)
            if diagonal:
                z = jnp.where(causal, z, NEG_BIG)
            sp = jnp.maximum(z, 0.0) + jnp.log(1.0 + jnp.exp2(-jnp.abs(z))) * LOG2E
            after = jnp.dot(sp.astype(BF16), later,
                            preferred_element_type=F32)
            a = jnp.exp2(z - sp - after - used[p])
            new_acc.append(acc[p] + jnp.dot(a.astype(BF16), vs, preferred_element_type=F32))
            new_used.append(used[p] + (after[:, 0:1] + sp[:, 0:1]))
        return tuple(new_used), tuple(new_acc)

    def first_pair():
        r2 = lax.broadcasted_iota(jnp.int32, (blk, 2 * blk), 0)
        c2 = lax.broadcasted_iota(jnp.int32, (blk, 2 * blk), 1)
        allowed = c2 < r2 + blk
        start = pl.multiple_of((i - 1) * blk, blk)
        new_used, new_acc = [], []
        for p, ln in enumerate(lanes):
            kw = k_ref[pl.ds(start, 2 * blk), ln]
            vw = v_ref[pl.ds(start, 2 * blk), ln]
            z = lax.dot_general(qs[p], kw, (((1,), (1,)), ((), ())), preferred_element_type=F32)
            z = jnp.where(allowed, z, NEG_BIG)
            sp = jnp.maximum(z, 0.0) + jnp.log(1.0 + jnp.exp2(-jnp.abs(z))) * LOG2E
            sp_b = sp.astype(BF16)
            after_hi = jnp.dot(sp_b[:, blk:], later, preferred_element_type=F32)
            total_hi = after_hi[:, 0:1] + sp[:, blk:blk + 1]
            after_lo = jnp.dot(sp_b[:, :blk], later, preferred_element_type=F32) + total_hi
            after = jnp.concatenate([after_lo, after_hi], axis=1)
            a = jnp.exp2(z - sp - after)
            new_acc.append(jnp.dot(a.astype(BF16), vw, preferred_element_type=F32))
            new_used.append(after_lo[:, 0:1] + sp[:, 0:1])
        return tuple(new_used), tuple(new_acc)

    def finish(acc):
        for p, ln in enumerate(lanes):
            mu = jnp.mean(acc[p], axis=-1, keepdims=True)
            d = acc[p] - mu
            var = jnp.mean(d * d, axis=-1, keepdims=True)
            o_ref[:, ln] = (d * lax.rsqrt(var + LN_EPS) * g_ref[:, ln]).astype(o_ref.dtype)

    @pl.when(i == 0)
    def _():
        used0 = tuple(jnp.zeros((blk, 1), F32) for _ in lanes)
        acc0 = tuple(jnp.zeros((blk, hd), F32) for _ in lanes)
        _, acc = visit(0, used0, acc0, True)
        finish(acc)

    @pl.when(i > 0)
    def _():
        used, acc = first_pair()

        def more(state):
            n, used, _ = state
            least = functools.reduce(jnp.minimum, used)
            return jnp.logical_and(n < i - 1, jnp.min(least) < F32_EXP2_UNDERFLOW)

        def body(state):
            n, used, acc = state
            used, acc = visit(i - 2 - n, used, acc, False)
            return n + 1, used, acc

        _, _, acc = lax.while_loop(more, body, (jnp.int32(0), used, acc))
        finish(acc)


def _sb_attention(z_main, norm_g_sb):
    s = z_main.shape[0]
    w = SB_PACK * SB_HEAD_DIM
    groups = SB_HEADS // SB_PACK
    return pl.pallas_call(
        _sb_kernel,
        grid=(groups, s // SB_BLK),
        in_specs=[
            pl.BlockSpec((SB_BLK, w), lambda h, i: (i, h)),
            pl.BlockSpec((s, w), lambda h, i: (0, groups + h)),
            pl.BlockSpec((s, w), lambda h, i: (0, 2 * groups + h)),
            pl.BlockSpec((1, w), lambda h, i: (0, h)),
        ],
        out_specs=pl.BlockSpec((SB_BLK, w), lambda h, i: (i, h)),
        out_shape=jax.ShapeDtypeStruct((s, SB_HEADS * SB_HEAD_DIM), BF16),
        compiler_params=_params(("parallel", "arbitrary")),
        name="stickbreak_attn",
    )(z_main, z_main, z_main, norm_g_sb)


def _mlstm_kernel(zq_ref, zk_ref, zv_ref, zo_ref, gcol_ref, grow_ref, cw_ref, cb_ref,
                  gb_row_ref, gb_col_ref, ng_ref, y_ref, ubuf_ref, c_ref, m_ref):
    L = ML_CHUNK
    nqk = ML_HEADS * ML_QK_DIM
    step = pl.program_id(0)

    @pl.when(step == 0)
    def _():
        ubuf_ref[0:SUBLANES, :] = jnp.zeros((SUBLANES, 2 * nqk), F32)
        c_ref[...] = jnp.zeros_like(c_ref)
        m_ref[...] = jnp.zeros_like(m_ref)

    ubuf_ref[SUBLANES:SUBLANES + L, 0:nqk] = zq_ref[...].astype(F32)
    ubuf_ref[SUBLANES:SUBLANES + L, nqk:2 * nqk] = zk_ref[...].astype(F32)
    conv = cb_ref[...]
    for tap in range(CONV_WIDTH):
        off = SUBLANES - (CONV_WIDTH - 1) + tap
        conv = conv + ubuf_ref[off:off + L, :] * cw_ref[tap:tap + 1, :]
    ubuf_ref[0:SUBLANES, :] = ubuf_ref[L:L + SUBLANES, :]
    qk = conv * jax.nn.sigmoid(conv)

    gc = gcol_ref[...] + gb_row_ref[...]
    gr = grow_ref[...] + gb_col_ref[...]
    r_id = lax.broadcasted_iota(jnp.int32, (L, L), 0)
    c_id = lax.broadcasted_iota(jnp.int32, (L, L), 1)
    causal = c_id <= r_id
    tri = causal.astype(BF16)
    tri_t = (r_id <= c_id).astype(BF16)
    lfc_hi, lfc_lo = _split_bf16(_log_sigmoid(gc))
    lfr_hi, lfr_lo = _split_bf16(_log_sigmoid(gr))
    bc = (jnp.dot(tri, lfc_hi, preferred_element_type=F32)
          + jnp.dot(tri, lfc_lo, preferred_element_type=F32))
    br = (jnp.dot(lfr_hi, tri_t, preferred_element_type=F32)
          + jnp.dot(lfr_lo, tri_t, preferred_element_type=F32))

    lane = lax.broadcasted_iota(jnp.int32, (L, LANES), 1)
    ones_col = (lane == 0).astype(BF16)

    for h in range(ML_HEADS):
        b_col = bc[:, ML_HEADS + h:ML_HEADS + h + 1]
        i_col = gc[:, h:h + 1]
        b_row = br[ML_HEADS + h:ML_HEADS + h + 1, :]
        i_row = gr[h:h + 1, :]
        m_prev = m_ref[h][0:1, 0:1]

        log_d = jnp.where(causal, b_col + (i_row - b_row), NEG_BIG)
        log_g = b_col + m_prev
        m_t = jnp.maximum(jnp.max(log_d, axis=-1, keepdims=True), log_g)
        p = jnp.exp(log_d - m_t)
        g = jnp.exp(log_g - m_t)

        qh = (qk[:, h * ML_QK_DIM:(h + 1) * ML_QK_DIM] * (ML_QK_DIM ** -0.5)).astype(BF16)
        kf = qk[:, nqk + h * ML_QK_DIM:nqk + (h + 1) * ML_QK_DIM]
        kh = kf.astype(BF16)
        vext = jnp.concatenate([zv_ref[:, h * ML_V_DIM:(h + 1) * ML_V_DIM], ones_col], axis=-1)

        sc = lax.dot_general(qh, kh, (((1,), (1,)), ((), ())), preferred_element_type=F32) * p
        state = c_ref[h]
        num = (jnp.dot(sc.astype(BF16), vext, preferred_element_type=F32)
               + g * jnp.dot(qh, state.astype(BF16), preferred_element_type=F32))
        den = num[:, ML_V_DIM:ML_V_DIM + 1]
        hv = num[:, 0:ML_V_DIM] * (1.0 / jnp.maximum(jnp.abs(den), jnp.exp(-m_t)))

        b_last = b_col[L - 1:L, :]
        log_w = b_last - b_col + i_col
        m_new = jnp.maximum(b_last + m_prev, jnp.max(log_w, axis=0, keepdims=True))
        decay = jnp.exp(b_last + m_prev - m_new)
        kw = (kf * jnp.exp(log_w - m_new)).astype(BF16)
        c_ref[h] = decay * state + lax.dot_general(
            kw, vext, (((0,), (0,)), ((), ())), preferred_element_type=F32)
        m_ref[h] = jnp.broadcast_to(m_new, (SUBLANES, LANES))

        mu = jnp.mean(hv, axis=-1, keepdims=True)
        d = hv - mu
        var = jnp.mean(d * d, axis=-1, keepdims=True)
        hn = d * lax.rsqrt(var + LN_EPS) * ng_ref[:, h * ML_V_DIM:(h + 1) * ML_V_DIM]
        og = jax.nn.sigmoid(zo_ref[:, h * ML_V_DIM:(h + 1) * ML_V_DIM].astype(F32))
        y_ref[:, h * ML_V_DIM:(h + 1) * ML_V_DIM] = (og * hn).astype(y_ref.dtype)


def _mlstm(z_main, gcol, grow, conv_w, conv_b, gb_row, gb_col, norm_g_ml):
    s = z_main.shape[0]
    L = ML_CHUNK
    nqk = ML_HEADS * ML_QK_DIM
    nv = ML_HEADS * ML_V_DIM
    sbw = SB_HEADS * SB_HEAD_DIM
    q_blk = (3 * sbw) // nqk
    v_blk = (3 * sbw + 2 * nqk) // nv
    gr = grow.shape[0]
    full = lambda c: (0, 0)
    return pl.pallas_call(
        _mlstm_kernel,
        grid=(s // L,),
        in_specs=[
            pl.BlockSpec((L, nqk), lambda c: (c, q_blk)),
            pl.BlockSpec((L, nqk), lambda c: (c, q_blk + 1)),
            pl.BlockSpec((L, nv), lambda c: (c, v_blk)),
            pl.BlockSpec((L, nv), lambda c: (c, v_blk + 1)),
            pl.BlockSpec((L, LANES), lambda c: (c, 0)),
            pl.BlockSpec((gr, L), lambda c: (0, c)),
            pl.BlockSpec((CONV_WIDTH, 2 * nqk), full),
            pl.BlockSpec((1, 2 * nqk), full),
            pl.BlockSpec((1, LANES), full),
            pl.BlockSpec((gr, 1), full),
            pl.BlockSpec((1, nv), full),
        ],
        out_specs=pl.BlockSpec((L, nv), lambda c: (c, 0)),
        out_shape=jax.ShapeDtypeStruct((s, nv), BF16),
        scratch_shapes=[
            pltpu.VMEM((L + SUBLANES, 2 * nqk), F32),
            pltpu.VMEM((ML_HEADS, ML_QK_DIM, ML_V_DIM + LANES), F32),
            pltpu.VMEM((ML_HEADS, SUBLANES, LANES), F32),
        ],
        compiler_params=_params(("arbitrary",)),
        name="mlstm_scan",
    )(z_main, z_main, z_main, z_main, gcol, grow, conv_w, conv_b, gb_row, gb_col, norm_g_ml)


def _outproj_kernel(x_ref, ysb_ref, yml_ref, wsb_ref, wml_ref, gate_ref, g_ref, b_ref, o_ref,
                    *, alpha):
    m = (jnp.dot(ysb_ref[...], wsb_ref[...], preferred_element_type=F32)
         + jnp.dot(yml_ref[...], wml_ref[...], preferred_element_type=F32))
    y = alpha * x_ref[...] + (1.0 + gate_ref[...]) * m
    o_ref[...] = _layer_norm_rows(y, g_ref[...], b_ref[...])


def _outproj(x, y_sb, y_ml, w_out, layer, gate, g, b, alpha):
    s, d = x.shape
    sbw = y_sb.shape[1]
    assert y_ml.shape[1] == sbw and w_out.shape[1] == 2 * sbw
    row = lambda i: (i, 0)
    full = lambda i: (0, 0)
    return pl.pallas_call(
        functools.partial(_outproj_kernel, alpha=alpha),
        grid=(s // OUT_TM,),
        in_specs=[
            pl.BlockSpec((OUT_TM, d), row),
            pl.BlockSpec((OUT_TM, y_sb.shape[1]), row),
            pl.BlockSpec((OUT_TM, y_ml.shape[1]), row),
            pl.BlockSpec((None, sbw, d), lambda i: (layer, 0, 0)),
            pl.BlockSpec((None, sbw, d), lambda i: (layer, 1, 0)),
            pl.BlockSpec((1, d), full),
            pl.BlockSpec((1, d), full),
            pl.BlockSpec((1, d), full),
        ],
        out_specs=pl.BlockSpec((OUT_TM, d), row),
        out_shape=jax.ShapeDtypeStruct((s, d), F32),
        compiler_params=_params(("parallel",)),
        name="mixer_out_proj",
    )(x, y_sb, y_ml, w_out, w_out, gate, g, b)


def kernel(x, c, ada_w, ada_b, ln_g, ln_b, ffn1_w1, ffn1_w2, mix_w_in, mlstm_conv_w,
           mlstm_conv_b, mlstm_gate_b, mix_norm_g, mix_w_out, ffn2_w1, ffn2_w2):
    bsz, s, d = x.shape
    depth = ada_w.shape[0]
    assert bsz == 1, "kernels are written for a single sequence"
    assert s % max(FFN_TM, PROJ_TM, SB_BLK, ML_CHUNK, OUT_TM) == 0
    alpha = (2 * depth) ** 0.25
    sbw = SB_HEADS * SB_HEAD_DIM
    n_main = mix_w_in.shape[2] - 2 * ML_HEADS
    gate_rows = 2 * SUBLANES

    xs = x.reshape(s, d)
    mod = _adaln(c.reshape(d, 1), ada_w, ada_b).reshape(depth, N_SUB, N_MOD, 1, d)
    col_scale = jnp.concatenate(
        [jnp.full((1, sbw), SB_HEAD_DIM ** -0.5 * LOG2E, F32), jnp.ones((1, n_main - sbw), F32)],
        axis=1)

    f1w1, f1w2 = ffn1_w1.astype(BF16), ffn1_w2.astype(BF16)
    f2w1, f2w2 = ffn2_w1.astype(BF16), ffn2_w2.astype(BF16)
    w_gates = mix_w_in[:, :, n_main:].astype(BF16)
    w_out = mix_w_out.astype(BF16)

    for l in range(depth):
        xs = _ffn(xs, mod[l, 0, 0], mod[l, 0, 1], mod[l, 0, 2], f1w1, f1w2, l,
                  ln_g[l, 0][None], ln_b[l, 0][None], alpha)

        w_gate = w_gates[l]
        w_gate_c = jnp.pad(w_gate, ((0, 0), (0, LANES - 2 * ML_HEADS)))
        w_gate_t = jnp.pad(w_gate.T, ((0, gate_rows - 2 * ML_HEADS), (0, 0)))
        z_main, gcol, grow = _proj(xs, mod[l, 1, 0], mod[l, 1, 1], mix_w_in, l, col_scale,
                                   w_gate_c, w_gate_t)
        gb = mlstm_gate_b[l]
        gb_row = jnp.pad(gb, (0, LANES - 2 * ML_HEADS))[None]
        gb_col = jnp.pad(gb, (0, gate_rows - 2 * ML_HEADS))[:, None]
        y_sb = _sb_attention(z_main, mix_norm_g[l][None, :sbw])
        y_ml = _mlstm(z_main, gcol, grow, mlstm_conv_w[l], mlstm_conv_b[l][None],
                      gb_row, gb_col, mix_norm_g[l][None, sbw:])
        xs = _outproj(xs, y_sb, y_ml, w_out, l, mod[l, 1, 2],
                      ln_g[l, 1][None], ln_b[l, 1][None], alpha)

        xs = _ffn(xs, mod[l, 2, 0], mod[l, 2, 1], mod[l, 2, 2], f2w1, f2w2, l,
                  ln_g[l, 2][None], ln_b[l, 2][None], alpha)

    return xs.reshape(bsz, s, d)
```

```python
import functools

import jax
import jax.numpy as jnp
from jax import lax
from jax.experimental import pallas as pl
from jax.experimental.pallas import tpu as pltpu

F32 = jnp.float32
BF16 = jnp.bfloat16

SB_HEADS = 8
SB_HEAD_DIM = 128
ML_HEADS = 4
ML_QK_DIM = 128
ML_V_DIM = 256
CONV_WIDTH = 4
N_SUB = 3
N_MOD = 3
LN_EPS = 1e-5
LANES = 128
SUBLANES = 8
NEG_BIG = -1e30
LOG2E = 1.4426950408889634
F32_EXP2_UNDERFLOW = 150.0

ADA_TN = 1024
FFN_TM = 1024
FFN_TF = 512
PROJ_TM = 1024
PROJ_TN = 1024
SB_BLK = 256
SB_PACK = 2
ML_CHUNK = 256
OUT_TM = 512
VMEM_LIMIT = 56 * 1024 * 1024
FFN_VMEM_LIMIT = 60 * 1024 * 1024


def _params(sem, vmem_limit=VMEM_LIMIT):
    return pltpu.CompilerParams(dimension_semantics=sem, vmem_limit_bytes=vmem_limit)


def _layer_norm_rows(y, g, b):
    mu = jnp.mean(y, axis=-1, keepdims=True)
    d = y - mu
    var = jnp.mean(d * d, axis=-1, keepdims=True)
    return d * lax.rsqrt(var + LN_EPS) * g + b


def _split_bf16(v):
    hi = v.astype(BF16)
    lo = (v - hi.astype(F32)).astype(BF16)
    return hi, lo


def _log_sigmoid(z):
    return jnp.minimum(z, 0.0) - jnp.log(1.0 + jnp.exp(-jnp.abs(z)))


def _adaln_kernel(c_ref, w_ref, b_ref, o_ref):
    cv = c_ref[...]
    sc = cv * jax.nn.sigmoid(cv)
    o_ref[0] = jnp.sum(w_ref[0] * sc, axis=0, keepdims=True) + b_ref[0]


def _adaln(c_col, ada_w, ada_b):
    depth, d, n = ada_w.shape
    return pl.pallas_call(
        _adaln_kernel,
        grid=(depth, n // ADA_TN),
        in_specs=[
            pl.BlockSpec((d, 1), lambda l, j: (0, 0)),
            pl.BlockSpec((1, d, ADA_TN), lambda l, j: (l, 0, j)),
            pl.BlockSpec((1, 1, ADA_TN), lambda l, j: (l, 0, j)),
        ],
        out_specs=pl.BlockSpec((1, 1, ADA_TN), lambda l, j: (l, 0, j)),
        out_shape=jax.ShapeDtypeStruct((depth, 1, n), F32),
        compiler_params=_params(("arbitrary", "arbitrary")),
        name="adaln_gemv",
    )(c_col, ada_w, ada_b.reshape(depth, 1, n))


def _ffn_kernel(x_ref, shift_ref, scale_ref, gate_ref, w1a_ref, w1u_ref, w2_ref,
                g_ref, b_ref, o_ref, *, alpha, n_f):
    j = pl.program_id(1)

    @pl.when(j == 0)
    def _():
        o_ref[...] = jnp.zeros_like(o_ref)

    h = (x_ref[...] * (1.0 + scale_ref[...]) + shift_ref[...]).astype(BF16)
    a = jnp.dot(h, w1a_ref[...], preferred_element_type=F32)
    u = jnp.dot(h, w1u_ref[...], preferred_element_type=F32)
    act = (a * jax.nn.sigmoid(a) * u).astype(BF16)
    o_ref[...] += jnp.dot(act, w2_ref[...], preferred_element_type=F32)

    @pl.when(j == n_f - 1)
    def _():
        y = alpha * x_ref[...] + (0.5 * (1.0 + gate_ref[...])) * o_ref[...]
        o_ref[...] = _layer_norm_rows(y, g_ref[...], b_ref[...])


def _ffn(x, shift, scale, gate, w1, w2, layer, g, b, alpha):
    s, d = x.shape
    d_ff = w2.shape[1]
    n_f = d_ff // FFN_TF
    row = lambda i, j: (i, 0)
    vec = lambda i, j: (0, 0)
    return pl.pallas_call(
        functools.partial(_ffn_kernel, alpha=alpha, n_f=n_f),
        grid=(s // FFN_TM, n_f),
        in_specs=[
            pl.BlockSpec((FFN_TM, d), row),
            pl.BlockSpec((1, d), vec),
            pl.BlockSpec((1, d), vec),
            pl.BlockSpec((1, d), vec),
            pl.BlockSpec((None, d, FFN_TF), lambda i, j: (layer, 0, j)),
            pl.BlockSpec((None, d, FFN_TF), lambda i, j: (layer, 0, j + n_f)),
            pl.BlockSpec((None, FFN_TF, d), lambda i, j: (layer, j, 0)),
            pl.BlockSpec((1, d), vec),
            pl.BlockSpec((1, d), vec),
        ],
        out_specs=pl.BlockSpec((FFN_TM, d), row),
        out_shape=jax.ShapeDtypeStruct((s, d), F32),
        compiler_params=_params(("parallel", "arbitrary"), FFN_VMEM_LIMIT),
        name="ffn_sublayer",
    )(x, shift, scale, gate, w1, w1, w2, g, b)


def _proj_kernel(x_ref, shift_ref, scale_ref, w_ref, cs_ref, wg_ref, wgt_ref,
                 z_ref, gcol_ref, grow_ref, h_ref):
    j = pl.program_id(1)

    @pl.when(j == 0)
    def _():
        h = (x_ref[...] * (1.0 + scale_ref[...]) + shift_ref[...]).astype(BF16)
        h_ref[...] = h
        gcol_ref[...] = jnp.dot(h, wg_ref[...], preferred_element_type=F32)
        grow_ref[...] = lax.dot_general(wgt_ref[...], h, (((1,), (1,)), ((), ())),
                                        preferred_element_type=F32)

    z = jnp.dot(h_ref[...], w_ref[...], preferred_element_type=F32)
    z_ref[...] = (z * cs_ref[...]).astype(BF16)


def _proj(x, shift, scale, w_in, layer, col_scale, w_gate_t):
    s, d = x.shape
    n = col_scale.shape[1]
    assert w_in.shape[2] == n + LANES
    gr = w_gate_t.shape[0]
    row = lambda i, j: (i, 0)
    vec = lambda i, j: (0, 0)
    return pl.pallas_call(
        _proj_kernel,
        grid=(s // PROJ_TM, n // PROJ_TN),
        in_specs=[
            pl.BlockSpec((PROJ_TM, d), row),
            pl.BlockSpec((1, d), vec),
            pl.BlockSpec((1, d), vec),
            pl.BlockSpec((None, d, PROJ_TN), lambda i, j: (layer, 0, j)),
            pl.BlockSpec((1, PROJ_TN), lambda i, j: (0, j)),
            pl.BlockSpec((None, d, LANES), lambda i, j: (layer, 0, n // LANES)),
            pl.BlockSpec((gr, d), vec),
        ],
        out_specs=[
            pl.BlockSpec((PROJ_TM, PROJ_TN), lambda i, j: (i, j)),
            pl.BlockSpec((PROJ_TM, LANES), row),
            pl.BlockSpec((gr, PROJ_TM), lambda i, j: (0, i)),
        ],
        out_shape=[
            jax.ShapeDtypeStruct((s, n), BF16),
            jax.ShapeDtypeStruct((s, LANES), F32),
            jax.ShapeDtypeStruct((gr, s), F32),
        ],
        scratch_shapes=[pltpu.VMEM((PROJ_TM, d), BF16)],
        compiler_params=_params(("parallel", "arbitrary")),
        name="mixer_in_proj",
    )(x, shift, scale, w_in, col_scale, w_in, w_gate_t)


def _sb_kernel(q_ref, k_ref, v_ref, g_ref, o_ref):
    blk = SB_BLK
    hd = SB_HEAD_DIM
    i = pl.program_id(1)
    lanes = [slice(p * hd, (p + 1) * hd) for p in range(SB_PACK)]
    qs = [q_ref[:, ln] for ln in lanes]
    r_id = lax.broadcasted_iota(jnp.int32, (blk, blk), 0)
    c_id = lax.broadcasted_iota(jnp.int32, (blk, blk), 1)
    later = (r_id > c_id).astype(BF16)
    causal = c_id < r_id

    def visit(j, used, acc, diagonal):
        start = pl.multiple_of(j * blk, blk)
        new_used, new_acc = [], []
        for p, ln in enumerate(lanes):
            ks = k_ref[pl.ds(start, blk), ln]
            vs = v_ref[pl.ds(start, blk), ln]
            z = lax.dot_general(qs[p], ks, (((1,), (1,)), ((), ())), preferred_element_type=F32)
            if diagonal:
                z = jnp.where(causal, z, NEG_BIG)
            sp = jnp.maximum(z, 0.0) + jnp.log(1.0 + jnp.exp2(-jnp.abs(z))) * LOG2E
            after = jnp.dot(sp.astype(BF16), later,
                            preferred_element_type=F32)
            a = jnp.exp2(z - sp - after - used[p])
            new_acc.append(acc[p] + jnp.dot(a.astype(BF16), vs, preferred_element_type=F32))
            new_used.append(used[p] + (after[:, 0:1] + sp[:, 0:1]))
        return tuple(new_used), tuple(new_acc)

    def first_pair():
        r2 = lax.broadcasted_iota(jnp.int32, (blk, 2 * blk), 0)
        c2 = lax.broadcasted_iota(jnp.int32, (blk, 2 * blk), 1)
        allowed = c2 < r2 + blk
        start = pl.multiple_of((i - 1) * blk, blk)
        new_used, new_acc = [], []
        for p, ln in enumerate(lanes):
            kw = k_ref[pl.ds(start, 2 * blk), ln]
            vw = v_ref[pl.ds(start, 2 * blk), ln]
            z = lax.dot_general(qs[p], kw, (((1,), (1,)), ((), ())), preferred_element_type=F32)
            z = jnp.where(allowed, z, NEG_BIG)
            sp = jnp.maximum(z, 0.0) + jnp.log(1.0 + jnp.exp2(-jnp.abs(z))) * LOG2E
            sp_b = sp.astype(BF16)
            after_hi = jnp.dot(sp_b[:, blk:], later, preferred_element_type=F32)
            total_hi = after_hi[:, 0:1] + sp[:, blk:blk + 1]
            after_lo = jnp.dot(sp_b[:, :blk], later, preferred_element_type=F32) + total_hi
            after = jnp.concatenate([after_lo, after_hi], axis=1)
            a = jnp.exp2(z - sp - after)
            new_acc.append(jnp.dot(a.astype(BF16), vw, preferred_element_type=F32))
            new_used.append(after_lo[:, 0:1] + sp[:, 0:1])
        return tuple(new_used), tuple(new_acc)

    def finish(acc):
        for p, ln in enumerate(lanes):
            mu = jnp.mean(acc[p], axis=-1, keepdims=True)
            d = acc[p] - mu
            var = jnp.mean(d * d, axis=-1, keepdims=True)
            o_ref[:, ln] = (d * lax.rsqrt(var + LN_EPS) * g_ref[:, ln]).astype(o_ref.dtype)

    @pl.when(i == 0)
    def _():
        used0 = tuple(jnp.zeros((blk, 1), F32) for _ in lanes)
        acc0 = tuple(jnp.zeros((blk, hd), F32) for _ in lanes)
        _, acc = visit(0, used0, acc0, True)
        finish(acc)

    @pl.when(i > 0)
    def _():
        used, acc = first_pair()

        def more(state):
            n, used, _ = state
            least = functools.reduce(jnp.minimum, used)
            return jnp.logical_and(n < i - 1, jnp.min(least) < F32_EXP2_UNDERFLOW)

        def body(state):
            n, used, acc = state
            used, acc = visit(i - 2 - n, used, acc, False)
            return n + 1, used, acc

        _, _, acc = lax.while_loop(more, body, (jnp.int32(0), used, acc))
        finish(acc)


def _sb_attention(z_main, norm_g_sb):
    s = z_main.shape[0]
    w = SB_PACK * SB_HEAD_DIM
    groups = SB_HEADS // SB_PACK
    return pl.pallas_call(
        _sb_kernel,
        grid=(groups, s // SB_BLK),
        in_specs=[
            pl.BlockSpec((SB_BLK, w), lambda h, i: (i, h)),
            pl.BlockSpec((s, w), lambda h, i: (0, groups + h)),
            pl.BlockSpec((s, w), lambda h, i: (0, 2 * groups + h)),
            pl.BlockSpec((1, w), lambda h, i: (0, h)),
        ],
        out_specs=pl.BlockSpec((SB_BLK, w), lambda h, i: (i, h)),
        out_shape=jax.ShapeDtypeStruct((s, SB_HEADS * SB_HEAD_DIM), BF16),
        compiler_params=_params(("parallel", "arbitrary")),
        name="stickbreak_attn",
    )(z_main, z_main, z_main, norm_g_sb)


def _mlstm_kernel(zq_ref, zk_ref, zv_ref, zo_ref, gcol_ref, grow_ref, cw_ref, cb_ref,
                  gb_row_ref, gb_col_ref, ng_ref, y_ref, ubuf_ref, c_ref, m_ref):
    L = ML_CHUNK
    nqk = ML_HEADS * ML_QK_DIM
    step = pl.program_id(0)

    @pl.when(step == 0)
    def _():
        ubuf_ref[0:SUBLANES, :] = jnp.zeros((SUBLANES, 2 * nqk), F32)
        c_ref[...] = jnp.zeros_like(c_ref)
        m_ref[...] = jnp.zeros_like(m_ref)

    ubuf_ref[SUBLANES:SUBLANES + L, 0:nqk] = zq_ref[...].astype(F32)
    ubuf_ref[SUBLANES:SUBLANES + L, nqk:2 * nqk] = zk_ref[...].astype(F32)
    conv = cb_ref[...]
    for tap in range(CONV_WIDTH):
        off = SUBLANES - (CONV_WIDTH - 1) + tap
        conv = conv + ubuf_ref[off:off + L, :] * cw_ref[tap:tap + 1, :]
    ubuf_ref[0:SUBLANES, :] = ubuf_ref[L:L + SUBLANES, :]
    qk = conv * jax.nn.sigmoid(conv)

    gc = gcol_ref[...] + gb_row_ref[...]
    gr = grow_ref[...] + gb_col_ref[...]
    r_id = lax.broadcasted_iota(jnp.int32, (L, L), 0)
    c_id = lax.broadcasted_iota(jnp.int32, (L, L), 1)
    causal = c_id <= r_id
    tri = causal.astype(BF16)
    tri_t = (r_id <= c_id).astype(BF16)
    lfc_hi, lfc_lo = _split_bf16(_log_sigmoid(gc))
    lfr_hi, lfr_lo = _split_bf16(_log_sigmoid(gr))
    bc = (jnp.dot(tri, lfc_hi, preferred_element_type=F32)
          + jnp.dot(tri, lfc_lo, preferred_element_type=F32))
    br = (jnp.dot(lfr_hi, tri_t, preferred_element_type=F32)
          + jnp.dot(lfr_lo, tri_t, preferred_element_type=F32))

    lane = lax.broadcasted_iota(jnp.int32, (L, LANES), 1)
    ones_col = (lane == 0).astype(BF16)

    for h in range(ML_HEADS):
        b_col = bc[:, ML_HEADS + h:ML_HEADS + h + 1]
        i_col = gc[:, h:h + 1]
        b_row = br[ML_HEADS + h:ML_HEADS + h + 1, :]
        i_row = gr[h:h + 1, :]
        m_prev = m_ref[h][0:1, 0:1]

        log_d = jnp.where(causal, b_col + (i_row - b_row), NEG_BIG)
        log_g = b_col + m_prev
        m_t = jnp.maximum(jnp.max(log_d, axis=-1, keepdims=True), log_g)
        p = jnp.exp(log_d - m_t)
        g = jnp.exp(log_g - m_t)

        qh = (qk[:, h * ML_QK_DIM:(h + 1) * ML_QK_DIM] * (ML_QK_DIM ** -0.5)).astype(BF16)
        kf = qk[:, nqk + h * ML_QK_DIM:nqk + (h + 1) * ML_QK_DIM]
        kh = kf.astype(BF16)
        vext = jnp.concatenate([zv_ref[:, h * ML_V_DIM:(h + 1) * ML_V_DIM], ones_col], axis=-1)

        sc = lax.dot_general(qh, kh, (((1,), (1,)), ((), ())), preferred_element_type=F32) * p
        state = c_ref[h]
        num = (jnp.dot(sc.astype(BF16), vext, preferred_element_type=F32)
               + g * jnp.dot(qh, state.astype(BF16), preferred_element_type=F32))
        den = num[:, ML_V_DIM:ML_V_DIM + 1]
        hv = num[:, 0:ML_V_DIM] * (1.0 / jnp.maximum(jnp.abs(den), jnp.exp(-m_t)))

        b_last = b_col[L - 1:L, :]
        log_w = b_last - b_col + i_col
        m_new = jnp.maximum(b_last + m_prev, jnp.max(log_w, axis=0, keepdims=True))
        decay = jnp.exp(b_last + m_prev - m_new)
        kw = (kf * jnp.exp(log_w - m_new)).astype(BF16)
        c_ref[h] = decay * state + lax.dot_general(
            kw, vext, (((0,), (0,)), ((), ())), preferred_element_type=F32)
        m_ref[h] = jnp.broadcast_to(m_new, (SUBLANES, LANES))

        mu = jnp.mean(hv, axis=-1, keepdims=True)
        d = hv - mu
        var = jnp.mean(d * d, axis=-1, keepdims=True)
        hn = d * lax.rsqrt(var + LN_EPS) * ng_ref[:, h * ML_V_DIM:(h + 1) * ML_V_DIM]
        og = jax.nn.sigmoid(zo_ref[:, h * ML_V_DIM:(h + 1) * ML_V_DIM].astype(F32))
        y_ref[:, h * ML_V_DIM:(h + 1) * ML_V_DIM] = (og * hn).astype(y_ref.dtype)


def _mlstm(z_main, gcol, grow, conv_w, conv_b, gb_row, gb_col, norm_g_ml):
    s = z_main.shape[0]
    L = ML_CHUNK
    nqk = ML_HEADS * ML_QK_DIM
    nv = ML_HEADS * ML_V_DIM
    sbw = SB_HEADS * SB_HEAD_DIM
    q_blk = (3 * sbw) // nqk
    v_blk = (3 * sbw + 2 * nqk) // nv
    gr = grow.shape[0]
    full = lambda c: (0, 0)
    return pl.pallas_call(
        _mlstm_kernel,
        grid=(s // L,),
        in_specs=[
            pl.BlockSpec((L, nqk), lambda c: (c, q_blk)),
            pl.BlockSpec((L, nqk), lambda c: (c, q_blk + 1)),
            pl.BlockSpec((L, nv), lambda c: (c, v_blk)),
            pl.BlockSpec((L, nv), lambda c: (c, v_blk + 1)),
            pl.BlockSpec((L, LANES), lambda c: (c, 0)),
            pl.BlockSpec((gr, L), lambda c: (0, c)),
            pl.BlockSpec((CONV_WIDTH, 2 * nqk), full),
            pl.BlockSpec((1, 2 * nqk), full),
            pl.BlockSpec((1, LANES), full),
            pl.BlockSpec((gr, 1), full),
            pl.BlockSpec((1, nv), full),
        ],
        out_specs=pl.BlockSpec((L, nv), lambda c: (c, 0)),
        out_shape=jax.ShapeDtypeStruct((s, nv), BF16),
        scratch_shapes=[
            pltpu.VMEM((L + SUBLANES, 2 * nqk), F32),
            pltpu.VMEM((ML_HEADS, ML_QK_DIM, ML_V_DIM + LANES), F32),
            pltpu.VMEM((ML_HEADS, SUBLANES, LANES), F32),
        ],
        compiler_params=_params(("arbitrary",)),
        name="mlstm_scan",
    )(z_main, z_main, z_main, z_main, gcol, grow, conv_w, conv_b, gb_row, gb_col, norm_g_ml)


def _outproj_kernel(x_ref, ysb_ref, yml_ref, wsb_ref, wml_ref, gate_ref, g_ref, b_ref, o_ref,
                    *, alpha):
    m = (jnp.dot(ysb_ref[...], wsb_ref[...], preferred_element_type=F32)
         + jnp.dot(yml_ref[...], wml_ref[...], preferred_element_type=F32))
    y = alpha * x_ref[...] + (1.0 + gate_ref[...]) * m
    o_ref[...] = _layer_norm_rows(y, g_ref[...], b_ref[...])


def _outproj(x, y_sb, y_ml, w_out, layer, gate, g, b, alpha):
    s, d = x.shape
    sbw = y_sb.shape[1]
    assert y_ml.shape[1] == sbw and w_out.shape[1] == 2 * sbw
    row = lambda i: (i, 0)
    full = lambda i: (0, 0)
    return pl.pallas_call(
        functools.partial(_outproj_kernel, alpha=alpha),
        grid=(s // OUT_TM,),
        in_specs=[
            pl.BlockSpec((OUT_TM, d), row),
            pl.BlockSpec((OUT_TM, y_sb.shape[1]), row),
            pl.BlockSpec((OUT_TM, y_ml.shape[1]), row),
            pl.BlockSpec((None, sbw, d), lambda i: (layer, 0, 0)),
            pl.BlockSpec((None, sbw, d), lambda i: (layer, 1, 0)),
            pl.BlockSpec((1, d), full),
            pl.BlockSpec((1, d), full),
            pl.BlockSpec((1, d), full),
        ],
        out_specs=pl.BlockSpec((OUT_TM, d), row),
        out_shape=jax.ShapeDtypeStruct((s, d), F32),
        compiler_params=_params(("parallel",)),
        name="mixer_out_proj",
    )(x, y_sb, y_ml, w_out, w_out, gate, g, b)


def kernel(x, c, ada_w, ada_b, ln_g, ln_b, ffn1_w1, ffn1_w2, mix_w_in, mlstm_conv_w,
           mlstm_conv_b, mlstm_gate_b, mix_norm_g, mix_w_out, ffn2_w1, ffn2_w2):
    bsz, s, d = x.shape
    depth = ada_w.shape[0]
    assert bsz == 1, "kernels are written for a single sequence"
    assert s % max(FFN_TM, PROJ_TM, SB_BLK, ML_CHUNK, OUT_TM) == 0
    alpha = (2 * depth) ** 0.25
    sbw = SB_HEADS * SB_HEAD_DIM
    n_main = mix_w_in.shape[2] - 2 * ML_HEADS
    gate_rows = 2 * SUBLANES

    xs = x.reshape(s, d)
    mod = _adaln(c.reshape(d, 1), ada_w, ada_b).reshape(depth, N_SUB, N_MOD, 1, d)
    col_scale = jnp.concatenate(
        [jnp.full((1, sbw), SB_HEAD_DIM ** -0.5 * LOG2E, F32), jnp.ones((1, n_main - sbw), F32)],
        axis=1)

    f1w1, f1w2 = ffn1_w1.astype(BF16), ffn1_w2.astype(BF16)
    f2w1, f2w2 = ffn2_w1.astype(BF16), ffn2_w2.astype(BF16)
    w_in = jnp.pad(mix_w_in, ((0, 0), (0, 0), (0, LANES - 2 * ML_HEADS))).astype(BF16)
    w_out = mix_w_out.astype(BF16)

    for l in range(depth):
        xs = _ffn(xs, mod[l, 0, 0], mod[l, 0, 1], mod[l, 0, 2], f1w1, f1w2, l,
                  ln_g[l, 0][None], ln_b[l, 0][None], alpha)

        w_gate_t = w_in[l, :, n_main:n_main + gate_rows].T
        z_main, gcol, grow = _proj(xs, mod[l, 1, 0], mod[l, 1, 1], w_in, l, col_scale, w_gate_t)
        gb = mlstm_gate_b[l]
        gb_row = jnp.pad(gb, (0, LANES - 2 * ML_HEADS))[None]
        gb_col = jnp.pad(gb, (0, gate_rows - 2 * ML_HEADS))[:, None]
        y_sb = _sb_attention(z_main, mix_norm_g[l][None, :sbw])
        y_ml = _mlstm(z_main, gcol, grow, mlstm_conv_w[l], mlstm_conv_b[l][None],
                      gb_row, gb_col, mix_norm_g[l][None, sbw:])
        xs = _outproj(xs, y_sb, y_ml, w_out, l, mod[l, 1, 2],
                      ln_g[l, 1][None], ln_b[l, 1][None], alpha)

        xs = _ffn(xs, mod[l, 2, 0], mod[l, 2, 1], mod[l, 2, 2], f2w1, f2w2, l,
                  ln_g[l, 2][None], ln_b[l, 2][None], alpha)

    return xs.reshape(bsz, s, d)
```

```python
import functools

import jax
import jax.numpy as jnp
from jax import lax
from jax.experimental import pallas as pl
from jax.experimental.pallas import tpu as pltpu

F32 = jnp.float32
BF16 = jnp.bfloat16

SB_HEADS = 8
SB_HEAD_DIM = 128
ML_HEADS = 4
ML_QK_DIM = 128
ML_V_DIM = 256
CONV_WIDTH = 4
N_SUB = 3
N_MOD = 3
LN_EPS = 1e-5
LANES = 128
SUBLANES = 8
NEG_BIG = -1e30
LOG2E = 1.4426950408889634
F32_EXP2_UNDERFLOW = 150.0

ADA_TN = 1024
FFN_TM = 1024
FFN_TF = 512
PROJ_TM = 1024
PROJ_TN = 1024
SB_BLK = 256
SB_PACK = 2
ML_CHUNK = 256
OUT_TM = 512
VMEM_LIMIT = 56 * 1024 * 1024
FFN_VMEM_LIMIT = 60 * 1024 * 1024


def _params(sem, vmem_limit=VMEM_LIMIT):
    return pltpu.CompilerParams(dimension_semantics=sem, vmem_limit_bytes=vmem_limit)


def _layer_norm_rows(y, g, b):
    mu = jnp.mean(y, axis=-1, keepdims=True)
    d = y - mu
    var = jnp.mean(d * d, axis=-1, keepdims=True)
    return d * lax.rsqrt(var + LN_EPS) * g + b


def _split_bf16(v):
    hi = v.astype(BF16)
    lo = (v - hi.astype(F32)).astype(BF16)
    return hi, lo


def _log_sigmoid(z):
    return jnp.minimum(z, 0.0) - jnp.log(1.0 + jnp.exp(-jnp.abs(z)))


def _adaln_kernel(c_ref, w_ref, b_ref, o_ref):
    cv = c_ref[...]
    sc = cv * jax.nn.sigmoid(cv)
    o_ref[0] = jnp.sum(w_ref[0] * sc, axis=0, keepdims=True) + b_ref[0]


def _adaln(c_col, ada_w, ada_b):
    depth, d, n = ada_w.shape
    return pl.pallas_call(
        _adaln_kernel,
        grid=(depth, n // ADA_TN),
        in_specs=[
            pl.BlockSpec((d, 1), lambda l, j: (0, 0)),
            pl.BlockSpec((1, d, ADA_TN), lambda l, j: (l, 0, j)),
            pl.BlockSpec((1, 1, ADA_TN), lambda l, j: (l, 0, j)),
        ],
        out_specs=pl.BlockSpec((1, 1, ADA_TN), lambda l, j: (l, 0, j)),
        out_shape=jax.ShapeDtypeStruct((depth, 1, n), F32),
        compiler_params=_params(("arbitrary", "arbitrary")),
        name="adaln_gemv",
    )(c_col, ada_w, ada_b.reshape(depth, 1, n))


def _ffn_kernel(x_ref, shift_ref, scale_ref, gate_ref, w1a_ref, w1u_ref, w2_ref,
                g_ref, b_ref, o_ref, *, alpha, n_f):
    j = pl.program_id(1)

    @pl.when(j == 0)
    def _():
        o_ref[...] = jnp.zeros_like(o_ref)

    h = (x_ref[...] * (1.0 + scale_ref[...]) + shift_ref[...]).astype(BF16)
    a = jnp.dot(h, w1a_ref[...], preferred_element_type=F32)
    u = jnp.dot(h, w1u_ref[...], preferred_element_type=F32)
    act = (a * jax.nn.sigmoid(a) * u).astype(BF16)
    o_ref[...] += jnp.dot(act, w2_ref[...], preferred_element_type=F32)

    @pl.when(j == n_f - 1)
    def _():
        y = alpha * x_ref[...] + (0.5 * (1.0 + gate_ref[...])) * o_ref[...]
        o_ref[...] = _layer_norm_rows(y, g_ref[...], b_ref[...])


def _ffn(x, shift, scale, gate, w1, w2, layer, g, b, alpha):
    s, d = x.shape
    d_ff = w2.shape[1]
    n_f = d_ff // FFN_TF
    row = lambda i, j: (i, 0)
    vec = lambda i, j: (0, 0)
    return pl.pallas_call(
        functools.partial(_ffn_kernel, alpha=alpha, n_f=n_f),
        grid=(s // FFN_TM, n_f),
        in_specs=[
            pl.BlockSpec((FFN_TM, d), row),
            pl.BlockSpec((1, d), vec),
            pl.BlockSpec((1, d), vec),
            pl.BlockSpec((1, d), vec),
            pl.BlockSpec((None, d, FFN_TF), lambda i, j: (layer, 0, j)),
            pl.BlockSpec((None, d, FFN_TF), lambda i, j: (layer, 0, j + n_f)),
            pl.BlockSpec((None, FFN_TF, d), lambda i, j: (layer, j, 0)),
            pl.BlockSpec((1, d), vec),
            pl.BlockSpec((1, d), vec),
        ],
        out_specs=pl.BlockSpec((FFN_TM, d), row),
        out_shape=jax.ShapeDtypeStruct((s, d), F32),
        compiler_params=_params(("parallel", "arbitrary"), FFN_VMEM_LIMIT),
        name="ffn_sublayer",
    )(x, shift, scale, gate, w1, w1, w2, g, b)


def _proj_kernel(x_ref, shift_ref, scale_ref, w_ref, cs_ref, wg_ref, z_ref, gcol_ref, h_ref):
    j = pl.program_id(1)

    @pl.when(j == 0)
    def _():
        h = (x_ref[...] * (1.0 + scale_ref[...]) + shift_ref[...]).astype(BF16)
        h_ref[...] = h
        gcol_ref[...] = jnp.dot(h, wg_ref[...], preferred_element_type=F32)

    z = jnp.dot(h_ref[...], w_ref[...], preferred_element_type=F32)
    z_ref[...] = (z * cs_ref[...]).astype(BF16)


def _proj(x, shift, scale, w_in, layer, col_scale):
    s, d = x.shape
    n = col_scale.shape[1]
    assert w_in.shape[2] == n + LANES
    row = lambda i, j: (i, 0)
    vec = lambda i, j: (0, 0)
    return pl.pallas_call(
        _proj_kernel,
        grid=(s // PROJ_TM, n // PROJ_TN),
        in_specs=[
            pl.BlockSpec((PROJ_TM, d), row),
            pl.BlockSpec((1, d), vec),
            pl.BlockSpec((1, d), vec),
            pl.BlockSpec((None, d, PROJ_TN), lambda i, j: (layer, 0, j)),
            pl.BlockSpec((1, PROJ_TN), lambda i, j: (0, j)),
            pl.BlockSpec((None, d, LANES), lambda i, j: (layer, 0, n // LANES)),
        ],
        out_specs=[
            pl.BlockSpec((PROJ_TM, PROJ_TN), lambda i, j: (i, j)),
            pl.BlockSpec((PROJ_TM, LANES), row),
        ],
        out_shape=[
            jax.ShapeDtypeStruct((s, n), BF16),
            jax.ShapeDtypeStruct((s, LANES), F32),
        ],
        scratch_shapes=[pltpu.VMEM((PROJ_TM, d), BF16)],
        compiler_params=_params(("parallel", "arbitrary")),
        name="mixer_in_proj",
    )(x, shift, scale, w_in, col_scale, w_in)


def _sb_kernel(q_ref, k_ref, v_ref, g_ref, o_ref):
    blk = SB_BLK
    hd = SB_HEAD_DIM
    i = pl.program_id(1)
    lanes = [slice(p * hd, (p + 1) * hd) for p in range(SB_PACK)]
    qs = [q_ref[:, ln] for ln in lanes]
    r_id = lax.broadcasted_iota(jnp.int32, (blk, blk), 0)
    c_id = lax.broadcasted_iota(jnp.int32, (blk, blk), 1)
    later = (r_id > c_id).astype(BF16)
    causal = c_id < r_id

    def visit(j, used, acc, diagonal):
        start = pl.multiple_of(j * blk, blk)
        new_used, new_acc = [], []
        for p, ln in enumerate(lanes):
            ks = k_ref[pl.ds(start, blk), ln]
            vs = v_ref[pl.ds(start, blk), ln]
            z = lax.dot_general(qs[p], ks, (((1,), (1,)), ((), ())), preferred_element_type=F32)
            if diagonal:
                z = jnp.where(causal, z, NEG_BIG)
            sp = jnp.maximum(z, 0.0) + jnp.log(1.0 + jnp.exp2(-jnp.abs(z))) * LOG2E
            after = jnp.dot(sp.astype(BF16), later,
                            preferred_element_type=F32)
            a = jnp.exp2(z - sp - after - used[p])
            new_acc.append(acc[p] + jnp.dot(a.astype(BF16), vs, preferred_element_type=F32))
            new_used.append(used[p] + (after[:, 0:1] + sp[:, 0:1]))
        return tuple(new_used), tuple(new_acc)

    def first_pair():
        r2 = lax.broadcasted_iota(jnp.int32, (blk, 2 * blk), 0)
        c2 = lax.broadcasted_iota(jnp.int32, (blk, 2 * blk), 1)
        allowed = c2 < r2 + blk
        start = pl.multiple_of((i - 1) * blk, blk)
        new_used, new_acc = [], []
        for p, ln in enumerate(lanes):
            kw = k_ref[pl.ds(start, 2 * blk), ln]
            vw = v_ref[pl.ds(start, 2 * blk), ln]
            z = lax.dot_general(qs[p], kw, (((1,), (1,)), ((), ())), preferred_element_type=F32)
            z = jnp.where(allowed, z, NEG_BIG)
            sp = jnp.maximum(z, 0.0) + jnp.log(1.0 + jnp.exp2(-jnp.abs(z))) * LOG2E
            sp_b = sp.astype(BF16)
            after_hi = jnp.dot(sp_b[:, blk:], later, preferred_element_type=F32)
            total_hi = after_hi[:, 0:1] + sp[:, blk:blk + 1]
            after_lo = jnp.dot(sp_b[:, :blk], later, preferred_element_type=F32) + total_hi
            after = jnp.concatenate([after_lo, after_hi], axis=1)
            a = jnp.exp2(z - sp - after)
            new_acc.append(jnp.dot(a.astype(BF16), vw, preferred_element_type=F32))
            new_used.append(after_lo[:, 0:1] + sp[:, 0:1])
        return tuple(new_used), tuple(new_acc)

    def finish(acc):
        for p, ln in enumerate(lanes):
            mu = jnp.mean(acc[p], axis=-1, keepdims=True)
            d = acc[p] - mu
            var = jnp.mean(d * d, axis=-1, keepdims=True)
            o_ref[:, ln] = (d * lax.rsqrt(var + LN_EPS) * g_ref[:, ln]).astype(o_ref.dtype)

    @pl.when(i == 0)
    def _():
        used0 = tuple(jnp.zeros((blk, 1), F32) for _ in lanes)
        acc0 = tuple(jnp.zeros((blk, hd), F32) for _ in lanes)
        _, acc = visit(0, used0, acc0, True)
        finish(acc)

    @pl.when(i > 0)
    def _():
        used, acc = first_pair()

        def more(state):
            n, used, _ = state
            least = functools.reduce(jnp.minimum, used)
            return jnp.logical_and(n < i - 1, jnp.min(least) < F32_EXP2_UNDERFLOW)

        def body(state):
            n, used, acc = state
            used, acc = visit(i - 2 - n, used, acc, False)
            return n + 1, used, acc

        _, _, acc = lax.while_loop(more, body, (jnp.int32(0), used, acc))
        finish(acc)


def _sb_attention(z_main, norm_g_sb):
    s = z_main.shape[0]
    w = SB_PACK * SB_HEAD_DIM
    groups = SB_HEADS // SB_PACK
    return pl.pallas_call(
        _sb_kernel,
        grid=(groups, s // SB_BLK),
        in_specs=[
            pl.BlockSpec((SB_BLK, w), lambda h, i: (i, h)),
            pl.BlockSpec((s, w), lambda h, i: (0, groups + h)),
            pl.BlockSpec((s, w), lambda h, i: (0, 2 * groups + h)),
            pl.BlockSpec((1, w), lambda h, i: (0, h)),
        ],
        out_specs=pl.BlockSpec((SB_BLK, w), lambda h, i: (i, h)),
        out_shape=jax.ShapeDtypeStruct((s, SB_HEADS * SB_HEAD_DIM), BF16),
        compiler_params=_params(("parallel", "arbitrary")),
        name="stickbreak_attn",
    )(z_main, z_main, z_main, norm_g_sb)


def _mlstm_kernel(zq_ref, zk_ref, zv_ref, zo_ref, gcol_ref, cw_ref, cb_ref,
                  gb_row_ref, ng_ref, y_ref, ubuf_ref, c_ref, m_ref):
    L = ML_CHUNK
    nqk = ML_HEADS * ML_QK_DIM
    step = pl.program_id(0)

    @pl.when(step == 0)
    def _():
        ubuf_ref[0:SUBLANES, :] = jnp.zeros((SUBLANES, 2 * nqk), F32)
        c_ref[...] = jnp.zeros_like(c_ref)
        m_ref[...] = jnp.zeros_like(m_ref)

    ubuf_ref[SUBLANES:SUBLANES + L, 0:nqk] = zq_ref[...].astype(F32)
    ubuf_ref[SUBLANES:SUBLANES + L, nqk:2 * nqk] = zk_ref[...].astype(F32)
    conv = cb_ref[...]
    for tap in range(CONV_WIDTH):
        off = SUBLANES - (CONV_WIDTH - 1) + tap
        conv = conv + ubuf_ref[off:off + L, :] * cw_ref[tap:tap + 1, :]
    ubuf_ref[0:SUBLANES, :] = ubuf_ref[L:L + SUBLANES, :]
    qk = conv * jax.nn.sigmoid(conv)

    gc = gcol_ref[...] + gb_row_ref[...]
    gr = gc.T[0:2 * SUBLANES, :]
    r_id = lax.broadcasted_iota(jnp.int32, (L, L), 0)
    c_id = lax.broadcasted_iota(jnp.int32, (L, L), 1)
    causal = c_id <= r_id
    tri = causal.astype(BF16)
    tri_t = (r_id <= c_id).astype(BF16)
    lfc_hi, lfc_lo = _split_bf16(_log_sigmoid(gc))
    lfr_hi, lfr_lo = _split_bf16(_log_sigmoid(gr))
    bc = (jnp.dot(tri, lfc_hi, preferred_element_type=F32)
          + jnp.dot(tri, lfc_lo, preferred_element_type=F32))
    br = (jnp.dot(lfr_hi, tri_t, preferred_element_type=F32)
          + jnp.dot(lfr_lo, tri_t, preferred_element_type=F32))

    lane = lax.broadcasted_iota(jnp.int32, (L, LANES), 1)
    ones_col = (lane == 0).astype(BF16)

    for h in range(ML_HEADS):
        b_col = bc[:, ML_HEADS + h:ML_HEADS + h + 1]
        i_col = gc[:, h:h + 1]
        b_row = br[ML_HEADS + h:ML_HEADS + h + 1, :]
        i_row = gr[h:h + 1, :]
        m_prev = m_ref[h][0:1, 0:1]

        log_d = jnp.where(causal, b_col + (i_row - b_row), NEG_BIG)
        log_g = b_col + m_prev
        m_t = jnp.maximum(jnp.max(log_d, axis=-1, keepdims=True), log_g)
        p = jnp.exp(log_d - m_t)
        g = jnp.exp(log_g - m_t)

        qh = (qk[:, h * ML_QK_DIM:(h + 1) * ML_QK_DIM] * (ML_QK_DIM ** -0.5)).astype(BF16)
        kf = qk[:, nqk + h * ML_QK_DIM:nqk + (h + 1) * ML_QK_DIM]
        kh = kf.astype(BF16)
        vext = jnp.concatenate([zv_ref[:, h * ML_V_DIM:(h + 1) * ML_V_DIM], ones_col], axis=-1)

        sc = lax.dot_general(qh, kh, (((1,), (1,)), ((), ())), preferred_element_type=F32) * p
        state = c_ref[h]
        num = (jnp.dot(sc.astype(BF16), vext, preferred_element_type=F32)
               + g * jnp.dot(qh, state.astype(BF16), preferred_element_type=F32))
        den = num[:, ML_V_DIM:ML_V_DIM + 1]
        hv = num[:, 0:ML_V_DIM] * (1.0 / jnp.maximum(jnp.abs(den), jnp.exp(-m_t)))

        b_last = b_col[L - 1:L, :]
        log_w = b_last - b_col + i_col
        m_new = jnp.maximum(b_last + m_prev, jnp.max(log_w, axis=0, keepdims=True))
        decay = jnp.exp(b_last + m_prev - m_new)
        kw = (kf * jnp.exp(log_w - m_new)).astype(BF16)
        c_ref[h] = decay * state + lax.dot_general(
            kw, vext, (((0,), (0,)), ((), ())), preferred_element_type=F32)
        m_ref[h] = jnp.broadcast_to(m_new, (SUBLANES, LANES))

        mu = jnp.mean(hv, axis=-1, keepdims=True)
        d = hv - mu
        var = jnp.mean(d * d, axis=-1, keepdims=True)
        hn = d * lax.rsqrt(var + LN_EPS) * ng_ref[:, h * ML_V_DIM:(h + 1) * ML_V_DIM]
        og = jax.nn.sigmoid(zo_ref[:, h * ML_V_DIM:(h + 1) * ML_V_DIM].astype(F32))
        y_ref[:, h * ML_V_DIM:(h + 1) * ML_V_DIM] = (og * hn).astype(y_ref.dtype)


def _mlstm(z_main, gcol, conv_w, conv_b, gb_row, norm_g_ml):
    s = z_main.shape[0]
    L = ML_CHUNK
    nqk = ML_HEADS * ML_QK_DIM
    nv = ML_HEADS * ML_V_DIM
    sbw = SB_HEADS * SB_HEAD_DIM
    q_blk = (3 * sbw) // nqk
    v_blk = (3 * sbw + 2 * nqk) // nv
    full = lambda c: (0, 0)
    return pl.pallas_call(
        _mlstm_kernel,
        grid=(s // L,),
        in_specs=[
            pl.BlockSpec((L, nqk), lambda c: (c, q_blk)),
            pl.BlockSpec((L, nqk), lambda c: (c, q_blk + 1)),
            pl.BlockSpec((L, nv), lambda c: (c, v_blk)),
            pl.BlockSpec((L, nv), lambda c: (c, v_blk + 1)),
            pl.BlockSpec((L, LANES), lambda c: (c, 0)),
            pl.BlockSpec((CONV_WIDTH, 2 * nqk), full),
            pl.BlockSpec((1, 2 * nqk), full),
            pl.BlockSpec((1, LANES), full),
            pl.BlockSpec((1, nv), full),
        ],
        out_specs=pl.BlockSpec((L, nv), lambda c: (c, 0)),
        out_shape=jax.ShapeDtypeStruct((s, nv), BF16),
        scratch_shapes=[
            pltpu.VMEM((L + SUBLANES, 2 * nqk), F32),
            pltpu.VMEM((ML_HEADS, ML_QK_DIM, ML_V_DIM + LANES), F32),
            pltpu.VMEM((ML_HEADS, SUBLANES, LANES), F32),
        ],
        compiler_params=_params(("arbitrary",)),
        name="mlstm_scan",
    )(z_main, z_main, z_main, z_main, gcol, conv_w, conv_b, gb_row, norm_g_ml)


def _outproj_kernel(x_ref, ysb_ref, yml_ref, wsb_ref, wml_ref, gate_ref, g_ref, b_ref, o_ref,
                    *, alpha):
    m = (jnp.dot(ysb_ref[...], wsb_ref[...], preferred_element_type=F32)
         + jnp.dot(yml_ref[...], wml_ref[...], preferred_element_type=F32))
    y = alpha * x_ref[...] + (1.0 + gate_ref[...]) * m
    o_ref[...] = _layer_norm_rows(y, g_ref[...], b_ref[...])


def _outproj(x, y_sb, y_ml, w_out, layer, gate, g, b, alpha):
    s, d = x.shape
    sbw = y_sb.shape[1]
    assert y_ml.shape[1] == sbw and w_out.shape[1] == 2 * sbw
    row = lambda i: (i, 0)
    full = lambda i: (0, 0)
    return pl.pallas_call(
        functools.partial(_outproj_kernel, alpha=alpha),
        grid=(s // OUT_TM,),
        in_specs=[
            pl.BlockSpec((OUT_TM, d), row),
            pl.BlockSpec((OUT_TM, y_sb.shape[1]), row),
            pl.BlockSpec((OUT_TM, y_ml.shape[1]), row),
            pl.BlockSpec((None, sbw, d), lambda i: (layer, 0, 0)),
            pl.BlockSpec((None, sbw, d), lambda i: (layer, 1, 0)),
            pl.BlockSpec((1, d), full),
            pl.BlockSpec((1, d), full),
            pl.BlockSpec((1, d), full),
        ],
        out_specs=pl.BlockSpec((OUT_TM, d), row),
        out_shape=jax.ShapeDtypeStruct((s, d), F32),
        compiler_params=_params(("parallel",)),
        name="mixer_out_proj",
    )(x, y_sb, y_ml, w_out, w_out, gate, g, b)


def kernel(x, c, ada_w, ada_b, ln_g, ln_b, ffn1_w1, ffn1_w2, mix_w_in, mlstm_conv_w,
           mlstm_conv_b, mlstm_gate_b, mix_norm_g, mix_w_out, ffn2_w1, ffn2_w2):
    bsz, s, d = x.shape
    depth = ada_w.shape[0]
    assert bsz == 1, "kernels are written for a single sequence"
    assert s % max(FFN_TM, PROJ_TM, SB_BLK, ML_CHUNK, OUT_TM) == 0
    alpha = (2 * depth) ** 0.25
    sbw = SB_HEADS * SB_HEAD_DIM
    n_main = mix_w_in.shape[2] - 2 * ML_HEADS

    xs = x.reshape(s, d)
    mod = _adaln(c.reshape(d, 1), ada_w, ada_b).reshape(depth, N_SUB, N_MOD, 1, d)
    col_scale = jnp.concatenate(
        [jnp.full((1, sbw), SB_HEAD_DIM ** -0.5 * LOG2E, F32), jnp.ones((1, n_main - sbw), F32)],
        axis=1)

    f1w1, f1w2 = ffn1_w1.astype(BF16), ffn1_w2.astype(BF16)
    f2w1, f2w2 = ffn2_w1.astype(BF16), ffn2_w2.astype(BF16)
    w_in = jnp.pad(mix_w_in, ((0, 0), (0, 0), (0, LANES - 2 * ML_HEADS))).astype(BF16)
    w_out = mix_w_out.astype(BF16)

    for l in range(depth):
        xs = _ffn(xs, mod[l, 0, 0], mod[l, 0, 1], mod[l, 0, 2], f1w1, f1w2, l,
                  ln_g[l, 0][None], ln_b[l, 0][None], alpha)

        z_main, gcol = _proj(xs, mod[l, 1, 0], mod[l, 1, 1], w_in, l, col_scale)
        gb_row = jnp.pad(mlstm_gate_b[l], (0, LANES - 2 * ML_HEADS))[None]
        y_sb = _sb_attention(z_main, mix_norm_g[l][None, :sbw])
        y_ml = _mlstm(z_main, gcol, mlstm_conv_w[l], mlstm_conv_b[l][None],
                      gb_row, mix_norm_g[l][None, sbw:])
        xs = _outproj(xs, y_sb, y_ml, w_out, l, mod[l, 1, 2],
                      ln_g[l, 1][None], ln_b[l, 1][None], alpha)

        xs = _ffn(xs, mod[l, 2, 0], mod[l, 2, 1], mod[l, 2, 2], f2w1, f2w2, l,
                  ln_g[l, 2][None], ln_b[l, 2][None], alpha)

    return xs.reshape(bsz, s, d)
```
